```python
import jax, jax.numpy as jnp
from jax import lax
import numpy as np

D_MODEL = 1024
BATCH = 2
SEQ = 16384
DEPTH = 2

N_MIXERS = 2
N_POOL_LAYERS = (DEPTH + 1) // 2
N_MOBA_LAYERS = DEPTH // 2
POOL_WINDOWS = (2, 4, 8, 16)
N_POOL_GROUPS = len(POOL_WINDOWS)
POOL_GROUP_DIM = D_MODEL // N_POOL_GROUPS
N_HEADS = 16
HEAD_DIM = D_MODEL // N_HEADS
MOBA_BLOCK = 256
MOBA_TOPK = 3
Q_CHUNK = 32
N_EXPERT_GROUPS = 4
EXPERTS_PER_GROUP = 8
N_EXPERTS = N_EXPERT_GROUPS * EXPERTS_PER_GROUP
EXPERT_TOPK = 2
D_EXPERT = D_MODEL // 4
RMS_EPS = 1e-6

kernel_name = "hybrid_pool_moba_hmoe_adaln"


def rmsnorm(x, g):
    xf = x.astype(jnp.float32)
    r = lax.rsqrt(jnp.mean(xf * xf, axis=-1, keepdims=True) + RMS_EPS)
    return (xf * r).astype(x.dtype) * g


def alibi_slopes(n_heads):
    return 2.0 ** (-8.0 * jnp.arange(1, n_heads + 1, dtype=jnp.float32) / n_heads)


def pool_mixer(h, w_pool, pool_scale):
    B, S, D = h.shape
    hf = h.astype(jnp.float32)
    pos = jnp.arange(S)
    outs = []
    for g, w in enumerate(POOL_WINDOWS):
        hg = hf[..., g * POOL_GROUP_DIM:(g + 1) * POOL_GROUP_DIM]
        cs = lax.cumsum(hg, axis=1)
        prev = jnp.pad(cs, ((0, 0), (w, 0), (0, 0)))[:, :S]
        cnt = jnp.minimum(pos + 1, w).astype(jnp.float32)[None, :, None]
        outs.append((cs - prev) / cnt - hg)
    p = jnp.stack(outs, axis=2).astype(h.dtype)
    y = jnp.einsum('bsgc,gcd->bsgd', p, w_pool).reshape(B, S, D)
    return y * pool_scale


def moba_attention(h, w_qkv, w_o):
    B, S, D = h.shape
    H, Dh, BLK, QC = N_HEADS, HEAD_DIM, MOBA_BLOCK, Q_CHUNK
    qkv = h @ w_qkv
    q, k, v = jnp.split(qkv, 3, axis=-1)
    to_heads = lambda t: t.reshape(B, S, H, Dh).transpose(0, 2, 1, 3)
    q, k, v = to_heads(q), to_heads(k), to_heads(v)
    NB = -(-S // BLK)
    K = min(MOBA_TOPK, NB)
    S_pad = NB * BLK
    pad = ((0, 0), (0, 0), (0, S_pad - S), (0, 0))
    q, k, v = jnp.pad(q, pad), jnp.pad(k, pad), jnp.pad(v, pad)
    kb = k.reshape(B, H, NB, BLK, Dh)
    vb = v.reshape(B, H, NB, BLK, Dh)
    kmean = jnp.mean(kb.astype(jnp.float32), axis=3)
    slopes = alibi_slopes(H)
    scale = Dh ** -0.5
    n_chunks = S_pad // QC
    qc_all = q.reshape(B, H, n_chunks, QC, Dh).transpose(2, 0, 1, 3, 4)
    b_idx = jnp.arange(B)[:, None, None, None]
    h_idx = jnp.arange(H)[None, :, None, None]
    blk_ar = jnp.arange(BLK)

    def one_chunk(args):
        ci, q_c = args
        q_pos = ci * QC + jnp.arange(QC)
        own = (ci * QC) // BLK
        gate = jnp.einsum('bhqd,bhnd->bhqn', q_c.astype(jnp.float32), kmean)
        gate = jnp.where(jnp.arange(NB) < own, gate, -jnp.inf)
        _, sel = lax.top_k(gate, K)
        sel_valid = sel < own
        ks = kb[b_idx, h_idx, sel]
        vs = vb[b_idx, h_idx, sel]
        s_sel = jnp.einsum('bhqd,bhqkld->bhqkl', q_c, ks,
                           preferred_element_type=jnp.float32) * scale
        k_pos_sel = sel[..., None] * BLK + blk_ar
        dist_sel = (q_pos[None, None, :, None, None] - k_pos_sel).astype(jnp.float32)
        s_sel = s_sel - slopes[None, :, None, None, None] * dist_sel
        s_sel = jnp.where(sel_valid[..., None], s_sel, -jnp.inf).reshape(B, H, QC, K * BLK)
        k_own = lax.dynamic_slice_in_dim(kb, own, 1, axis=2)[:, :, 0]
        v_own = lax.dynamic_slice_in_dim(vb, own, 1, axis=2)[:, :, 0]
        s_own = jnp.einsum('bhqd,bhld->bhql', q_c, k_own,
                           preferred_element_type=jnp.float32) * scale
        dist_own = (q_pos[:, None] - (own * BLK + blk_ar)[None, :]).astype(jnp.float32)
        s_own = jnp.where(dist_own >= 0, s_own - slopes[None, :, None, None] * dist_own, -jnp.inf)
        p = jax.nn.softmax(jnp.concatenate([s_sel, s_own], axis=-1), axis=-1).astype(q_c.dtype)
        p_sel = p[..., :K * BLK].reshape(B, H, QC, K, BLK)
        p_own = p[..., K * BLK:]
        return (jnp.einsum('bhqkl,bhqkld->bhqd', p_sel, vs)
                + jnp.einsum('bhql,bhld->bhqd', p_own, v_own))

    o = lax.map(one_chunk, (jnp.arange(n_chunks), qc_all))
    o = o.transpose(1, 0, 3, 2, 4).reshape(B, S_pad, D)[:, :S]
    return o @ w_o


def hier_moe(h, w_group, w_expert, w_gate, w_up, w_down):
    B, S, D = h.shape
    t = h.reshape(-1, D)
    T = t.shape[0]
    g_prob = jax.nn.softmax((t @ w_group).astype(jnp.float32), axis=-1)
    g_w, g_sel = lax.top_k(g_prob, 1)
    e_logits = (t @ w_expert).astype(jnp.float32).reshape(T, N_EXPERT_GROUPS, EXPERTS_PER_GROUP)
    e_logits = e_logits[jnp.arange(T), g_sel[:, 0]]
    e_val, e_sel = lax.top_k(e_logits, EXPERT_TOPK)
    e_w = jax.nn.softmax(e_val, axis=-1) * g_w
    expert_id = g_sel * EXPERTS_PER_GROUP + e_sel
    combine = jnp.einsum('tke,tk->te', jax.nn.one_hot(expert_id, N_EXPERTS, dtype=jnp.float32),
                         e_w).astype(t.dtype)
    y = jnp.zeros_like(t)
    for e in range(N_EXPERTS):
        he = jax.nn.silu(t @ w_gate[e]) * (t @ w_up[e])
        y = y + combine[:, e, None] * (he @ w_down[e])
    return y.reshape(B, S, D)


def setup_inputs(seed: int = 0) -> dict:
    key = jax.random.key(seed)
    ks = jax.random.split(key, 20)
    D, F, E, G = D_MODEL, D_EXPERT, N_EXPERTS, N_EXPERT_GROUPS
    nrm = lambda k, shape, s: jax.random.normal(k, shape, jnp.float32) * s
    return {
        "x": nrm(ks[0], (BATCH, SEQ, D), 1.0),
        "c": nrm(ks[1], (BATCH, D), 1.0),
        "ada_w": nrm(ks[2], (DEPTH, D, 6 * D), D ** -0.5),
        "ada_b": nrm(ks[3], (DEPTH, 6 * D), 0.02),
        "norm_mix_g": 1.0 + nrm(ks[4], (DEPTH, D), 0.05),
        "norm_ffn_g": 1.0 + nrm(ks[5], (DEPTH, D), 0.05),
        "pool_w": nrm(ks[6], (N_POOL_LAYERS, N_POOL_GROUPS, POOL_GROUP_DIM, POOL_GROUP_DIM), POOL_GROUP_DIM ** -0.5),
        "pool_scale": 1.0 + nrm(ks[7], (N_POOL_LAYERS, D), 0.1),
        "w_qkv": nrm(ks[8], (N_MOBA_LAYERS, D, 3 * D), D ** -0.5),
        "w_o": nrm(ks[9], (N_MOBA_LAYERS, D, D), D ** -0.5),
        "router_group_w": nrm(ks[10], (DEPTH, D, G), D ** -0.5),
        "router_expert_w": nrm(ks[11], (DEPTH, D, E), D ** -0.5),
        "exp_w_gate": nrm(ks[12], (DEPTH, E, D, F), D ** -0.5),
        "exp_w_up": nrm(ks[13], (DEPTH, E, D, F), D ** -0.5),
        "exp_w_down": nrm(ks[14], (DEPTH, E, F, D), F ** -0.5),
        "final_norm_g": 1.0 + nrm(ks[15], (D,), 0.05),
    }


def reference(x, c, ada_w, ada_b, norm_mix_g, norm_ffn_g, pool_w, pool_scale, w_qkv, w_o,
              router_group_w, router_expert_w, exp_w_gate, exp_w_up, exp_w_down, final_norm_g):
    c_act = jax.nn.silu(c)
    for i in range(DEPTH):
        mod = c_act @ ada_w[i] + ada_b[i]
        sh_m, sc_m, g_m, sh_f, sc_f, g_f = [m[:, None, :] for m in jnp.split(mod, 6, axis=-1)]
        h = rmsnorm(x, norm_mix_g[i]) * (1.0 + sc_m) + sh_m
        if i % N_MIXERS == 0:
            y = pool_mixer(h, pool_w[i // N_MIXERS], pool_scale[i // N_MIXERS])
        else:
            y = moba_attention(h, w_qkv[i // N_MIXERS], w_o[i // N_MIXERS])
        x = x + g_m * y
        h = rmsnorm(x, norm_ffn_g[i]) * (1.0 + sc_f) + sh_f
        x = x + g_f * hier_moe(h, router_group_w[i], router_expert_w[i],
                               exp_w_gate[i], exp_w_up[i], exp_w_down[i])
    return rmsnorm(x, final_norm_g)
```

```python
import functools

import jax
import jax.numpy as jnp
from jax import lax
from jax.experimental import pallas as pl
from jax.experimental.pallas import tpu as pltpu

F32 = jnp.float32
BF16 = jnp.bfloat16
HIGHEST = lax.Precision.HIGHEST

RMS_EPS = 1e-6
POOL_WINDOWS = (2, 4, 8, 16)
POOL_HALO = 16
N_HEADS = 16
HEAD_DIM = 64
MOBA_BLOCK = 256
MOBA_TOPK = 3
N_EXPERT_GROUPS = 4
EXPERTS_PER_GROUP = 8
N_EXPERTS = N_EXPERT_GROUPS * EXPERTS_PER_GROUP
LANES = 128
NEG_BIG = -1e30
LOG2E = 1.4426950408889634
VMEM_LIMIT = 56 * 1024 * 1024


def _cparams(n_axes):
    return pltpu.CompilerParams(dimension_semantics=("arbitrary",) * n_axes,
                                vmem_limit_bytes=VMEM_LIMIT)


def _rms_mod(x, g, sc, sh):
    r = lax.rsqrt(jnp.mean(x * x, axis=-1, keepdims=True) + RMS_EPS)
    return (x * r) * g * (1.0 + sc) + sh


def _adaln_kernel(c_ref, w_ref, b_ref, o_ref):
    c = c_ref[...]
    ca = c / (1.0 + jnp.exp(-c))
    o_ref[0] = jnp.dot(ca, w_ref[0], precision=HIGHEST, preferred_element_type=F32) + b_ref[0]


def _adaln(c, ada_w, ada_b):
    depth, d, n = ada_w.shape
    b = c.shape[0]
    rows = 8
    cp = jnp.zeros((rows, d), F32).at[:b].set(c)
    tn = 1536
    out = pl.pallas_call(
        _adaln_kernel,
        grid=(depth, n // tn),
        in_specs=[pl.BlockSpec((rows, d), lambda i, j: (0, 0)),
                  pl.BlockSpec((1, d, tn), lambda i, j: (i, 0, j)),
                  pl.BlockSpec((1, 1, tn), lambda i, j: (i, 0, j))],
        out_specs=pl.BlockSpec((1, rows, tn), lambda i, j: (i, 0, j)),
        out_shape=jax.ShapeDtypeStruct((depth, rows, n), F32),
        compiler_params=_cparams(2),
        name="adaln",
    )(cp, ada_w, ada_b.reshape(depth, 1, n))
    return out[:, :b].reshape(depth, b, 6, d)


def _pool_kernel(x_ref, mod_ref, g_ref, w_ref, ps_ref, o_ref, hb_ref, *, ts):
    s = pl.program_id(1)

    @pl.when(s == 0)
    def _():
        hb_ref[0:POOL_HALO, :] = jnp.zeros((POOL_HALO, hb_ref.shape[1]), F32)

    x = x_ref[0]
    mod = mod_ref[0]
    h = _rms_mod(x, g_ref[...], mod[1:2], mod[0:1])
    hb_ref[POOL_HALO:POOL_HALO + ts, :] = h
    pos = s * ts + lax.broadcasted_iota(jnp.int32, (ts, 1), 0)
    gd = x.shape[1] // len(POOL_WINDOWS)
    ys = []
    for gi, w in enumerate(POOL_WINDOWS):
        c0 = gi * gd
        hg = h[:, c0:c0 + gd]
        acc = hg
        for k in range(1, w):
            acc = acc + hb_ref[POOL_HALO - k:POOL_HALO - k + ts, c0:c0 + gd]
        cnt = jnp.minimum(pos + 1, w).astype(F32)
        p = (acc / cnt - hg).astype(BF16)
        ys.append(jnp.dot(p, w_ref[gi], preferred_element_type=F32))
    y = jnp.concatenate(ys, axis=-1) * ps_ref[...]
    o_ref[0] = x + mod[2:3] * y
    hb_ref[0:POOL_HALO, :] = hb_ref[ts:ts + POOL_HALO, :]


def _pool_layer(x, mod, norm_g, pool_w, pool_scale):
    b, s, d = x.shape
    ts = 512
    g = len(POOL_WINDOWS)
    return pl.pallas_call(
        functools.partial(_pool_kernel, ts=ts),
        grid=(b, s // ts),
        in_specs=[pl.BlockSpec((1, ts, d), lambda bi, si: (bi, si, 0)),
                  pl.BlockSpec((1, 6, d), lambda bi, si: (bi, 0, 0)),
                  pl.BlockSpec((1, d), lambda bi, si: (0, 0)),
                  pl.BlockSpec((g, d // g, d // g), lambda bi, si: (0, 0, 0)),
                  pl.BlockSpec((1, d), lambda bi, si: (0, 0))],
        out_specs=pl.BlockSpec((1, ts, d), lambda bi, si: (bi, si, 0)),
        out_shape=jax.ShapeDtypeStruct((b, s, d), F32),
        scratch_shapes=[pltpu.VMEM((POOL_HALO + ts, d), F32)],
        compiler_params=_cparams(2),
        name="pool_mixer",
    )(x, mod, norm_g.reshape(1, d), pool_w.astype(BF16), pool_scale.reshape(1, d))


def _route(logits):
    lane = lax.broadcasted_iota(jnp.int32, logits.shape, 1)
    big = jnp.int32(1 << 20)
    is_g = (lane >= N_EXPERTS) & (lane < N_EXPERTS + N_EXPERT_GROUPS)
    glog = jnp.where(is_g, logits, -jnp.inf)
    gmax = jnp.max(glog, axis=-1, keepdims=True)
    gsel = jnp.min(jnp.where(glog == gmax, lane, big), axis=-1, keepdims=True) - N_EXPERTS
    gsum = jnp.sum(jnp.exp(glog - gmax), axis=-1, keepdims=True)
    g_w = 1.0 / gsum
    lo = gsel * EXPERTS_PER_GROUP
    in_grp = (lane >= lo) & (lane < lo + EXPERTS_PER_GROUP)
    elog = jnp.where(in_grp, logits, -jnp.inf)
    m1 = jnp.max(elog, axis=-1, keepdims=True)
    i1 = jnp.min(jnp.where(elog == m1, lane, big), axis=-1, keepdims=True)
    elog2 = jnp.where(lane == i1, -jnp.inf, elog)
    m2 = jnp.max(elog2, axis=-1, keepdims=True)
    i2 = jnp.min(jnp.where(elog2 == m2, lane, big), axis=-1, keepdims=True)
    d = jnp.exp(m2 - m1)
    w1 = g_w / (1.0 + d)
    w2 = g_w * d / (1.0 + d)
    return jnp.where(lane == i1, w1, 0.0) + jnp.where(lane == i2, w2, 0.0)


def _moe_kernel(x_ref, mod_ref, g_ref, wr_ref, wgu_ref, wd_ref, fg_ref, o_ref,
                h_ref, comb_ref, acc_ref, *, final_norm):
    e = pl.program_id(1)
    mod = mod_ref[0]

    @pl.when(e == 0)
    def _():
        h = _rms_mod(x_ref[...], g_ref[...], mod[4:5], mod[3:4])
        h_ref[...] = h.astype(BF16)
        logits = jnp.dot(h, wr_ref[...], precision=HIGHEST, preferred_element_type=F32)
        comb_ref[...] = _route(logits)
        acc_ref[...] = jnp.zeros_like(acc_ref)

    f = wd_ref.shape[1]
    gu = jnp.dot(h_ref[...], wgu_ref[0], preferred_element_type=F32)
    gt = gu[:, :f]
    act = (gt / (1.0 + jnp.exp(-gt))) * gu[:, f:]
    out = jnp.dot(act.astype(BF16), wd_ref[0], preferred_element_type=F32)
    comb = comb_ref[...]
    lane = lax.broadcasted_iota(jnp.int32, comb.shape, 1)
    ce = jnp.sum(jnp.where(lane == e, comb, 0.0), axis=-1, keepdims=True)
    acc_ref[...] += ce * out

    @pl.when(e == pl.num_programs(1) - 1)
    def _():
        xn = x_ref[...] + mod[5:6] * acc_ref[...]
        if final_norm:
            r = lax.rsqrt(jnp.mean(xn * xn, axis=-1, keepdims=True) + RMS_EPS)
            xn = (xn * r) * fg_ref[...]
        o_ref[...] = xn


def _moe_layer(x2, mod, norm_g, w_group, w_expert, w_gate, w_up, w_down, final_g, seq, final_norm):
    t, d = x2.shape
    e, _, f = w_gate.shape
    tm = 1024
    wr = jnp.zeros((d, LANES), F32).at[:, :e].set(w_expert).at[:, e:e + N_EXPERT_GROUPS].set(w_group)
    wgu = jnp.concatenate([w_gate, w_up], axis=-1).astype(BF16)
    wd = w_down.astype(BF16)
    return pl.pallas_call(
        functools.partial(_moe_kernel, final_norm=final_norm),
        grid=(t // tm, e),
        in_specs=[pl.BlockSpec((tm, d), lambda ti, ei: (ti, 0)),
                  pl.BlockSpec((1, 6, d), lambda ti, ei: (ti * tm // seq, 0, 0)),
                  pl.BlockSpec((1, d), lambda ti, ei: (0, 0)),
                  pl.BlockSpec((d, LANES), lambda ti, ei: (0, 0)),
                  pl.BlockSpec((1, d, 2 * f), lambda ti, ei: (ei, 0, 0)),
                  pl.BlockSpec((1, f, d), lambda ti, ei: (ei, 0, 0)),
                  pl.BlockSpec((1, d), lambda ti, ei: (0, 0))],
        out_specs=pl.BlockSpec((tm, d), lambda ti, ei: (ti, 0)),
        out_shape=jax.ShapeDtypeStruct((t, d), F32),
        scratch_shapes=[pltpu.VMEM((tm, d), BF16),
                        pltpu.VMEM((tm, LANES), F32),
                        pltpu.VMEM((tm, d), F32)],
        compiler_params=_cparams(2),
        name="moe_final" if final_norm else "moe",
    )(x2, mod, norm_g.reshape(1, d), wr, wgu, wd, final_g.reshape(1, d))


def _qkv_kernel(x_ref, mod_ref, g_ref, w_ref, q_ref, k_ref, v_ref):
    mod = mod_ref[0]
    h = _rms_mod(x_ref[0], g_ref[...], mod[1:2], mod[0:1]).astype(BF16)
    d = h.shape[1]
    q_ref[0] = jnp.dot(h, w_ref[:, 0:d], preferred_element_type=F32)
    k_ref[0] = jnp.dot(h, w_ref[:, d:2 * d], preferred_element_type=F32).astype(BF16)
    v_ref[0] = jnp.dot(h, w_ref[:, 2 * d:3 * d], preferred_element_type=F32).astype(BF16)


def _qkv(x, mod, norm_g, w_qkv):
    b, s, d = x.shape
    ts = 512
    blk = pl.BlockSpec((1, ts, d), lambda bi, si: (bi, si, 0))
    return pl.pallas_call(
        _qkv_kernel,
        grid=(b, s // ts),
        in_specs=[blk,
                  pl.BlockSpec((1, 6, d), lambda bi, si: (bi, 0, 0)),
                  pl.BlockSpec((1, d), lambda bi, si: (0, 0)),
                  pl.BlockSpec((d, 3 * d), lambda bi, si: (0, 0))],
        out_specs=[blk, blk, blk],
        out_shape=[jax.ShapeDtypeStruct((b, s, d), F32),
                   jax.ShapeDtypeStruct((b, s, d), BF16),
                   jax.ShapeDtypeStruct((b, s, d), BF16)],
        compiler_params=_cparams(2),
        name="qkv_proj",
    )(x, mod, norm_g.reshape(1, d), w_qkv.astype(BF16))


_SLOPE_LANE = 64
_BLKIDX_LANE = 67


def _attn_kernel(slopes_ref, q_ref, k_ref, v_ref, o_ref,
                 kmean_ref, kext_ref, v0_ref, v1_ref, lhs_ref, m_ref, acc_ref, *, nb):
    p = pl.program_id(1)
    i = pl.program_id(2)
    blk = MOBA_BLOCK
    lane = lax.broadcasted_iota(jnp.int32, (blk, LANES), 1)
    head0 = lane < HEAD_DIM

    @pl.when(i == 0)
    def _():
        kmean_ref[...] = jnp.zeros_like(kmean_ref)
        row = lax.broadcasted_iota(jnp.int32, (blk, LANES), 0).astype(F32)
        in_slope = (lane >= _SLOPE_LANE) & (lane < _SLOPE_LANE + 3)
        in_blk = (lane >= _BLKIDX_LANE) & (lane < _BLKIDX_LANE + 3)
        ones0 = jnp.where(lane == HEAD_DIM, 1.0, 0.0)
        ones1 = jnp.where(lane == 0, 1.0, 0.0)

        def build(j, carry):
            rows = pl.ds(pl.multiple_of(j * blk, blk), blk)
            kj = k_ref[0, rows, :]
            kmean_ref[pl.ds(j, 1), :] = jnp.mean(kj.astype(F32), axis=0, keepdims=True)
            jf = j.astype(F32)
            extra = jnp.where(lane == j, 1.0, jnp.where(in_slope, row, jnp.where(in_blk, jf, 0.0)))
            kext_ref[rows, 0:LANES] = kj
            kext_ref[rows, LANES:2 * LANES] = extra.astype(BF16)
            vj = v_ref[0, rows, :].astype(F32)
            v0_ref[rows, :] = jnp.where(head0, vj, ones0).astype(BF16)
            v1_ref[rows, :] = jnp.where(head0, ones1, vj).astype(BF16)
            return carry

        lax.fori_loop(0, nb, build, 0)

    q = q_ref[0] * (HEAD_DIM ** -0.5 * LOG2E)
    kmean = kmean_ref[...]
    valid = lane < i
    big = jnp.int32(1 << 20)
    for hh in range(2):
        qm = jnp.where(head0 if hh == 0 else ~head0, q, 0.0)
        gate = lax.dot_general(qm, kmean, (((1,), (1,)), ((), ())),
                               precision=HIGHEST, preferred_element_type=F32)
        gate = jnp.where(valid, gate, -jnp.inf)
        sel = lane == i
        for _ in range(MOBA_TOPK):
            mx = jnp.max(gate, axis=-1, keepdims=True)
            idx = jnp.min(jnp.where(gate == mx, lane, big), axis=-1, keepdims=True)
            hit = lane == idx
            sel = sel | (hit & valid)
            gate = jnp.where(hit, -jnp.inf, gate)
        slope = jnp.full((blk, LANES), slopes_ref[2 * p + hh] * LOG2E, F32)
        s_hi = slope.astype(BF16).astype(F32)
        rem = slope - s_hi
        s_mid = rem.astype(BF16).astype(F32)
        s_lo = rem - s_mid
        ext = jnp.where(sel, 0.0, NEG_BIG)
        for k, part in enumerate((s_hi, s_mid, s_lo)):
            ext = jnp.where(lane == _SLOPE_LANE + k, part, ext)
            ext = jnp.where(lane == _BLKIDX_LANE + k, part * float(blk), ext)
        ext = jnp.where(lane >= _BLKIDX_LANE + 3, 0.0, ext)
        lhs_ref[hh * blk:(hh + 1) * blk, 0:LANES] = qm.astype(BF16)
        lhs_ref[hh * blk:(hh + 1) * blk, LANES:2 * LANES] = ext.astype(BF16)

    m_ref[...] = jnp.full(m_ref.shape, NEG_BIG, F32)
    acc_ref[...] = jnp.zeros_like(acc_ref)

    def tile(j, causal):
        rows = pl.ds(pl.multiple_of(j * blk, blk), blk)
        s = lax.dot_general(lhs_ref[...], kext_ref[rows, :], (((1,), (1,)), ((), ())),
                            preferred_element_type=F32)
        if causal:
            qi = lax.broadcasted_iota(jnp.int32, s.shape, 0) & (blk - 1)
            ki = lax.broadcasted_iota(jnp.int32, s.shape, 1)
            s = jnp.where(ki <= qi, s, NEG_BIG)
        m_old = m_ref[...]
        m_new = jnp.maximum(m_old, jnp.max(s, axis=-1, keepdims=True))
        alpha = jnp.exp2(m_old - m_new)
        pm = jnp.exp2(s - m_new).astype(BF16)
        pv0 = jnp.dot(pm[0:blk], v0_ref[rows, :], preferred_element_type=F32)
        pv1 = jnp.dot(pm[blk:2 * blk], v1_ref[rows, :], preferred_element_type=F32)
        acc_ref[0:blk, :] = alpha[0:blk] * acc_ref[0:blk, :] + pv0
        acc_ref[blk:2 * blk, :] = alpha[blk:2 * blk] * acc_ref[blk:2 * blk, :] + pv1
        m_ref[...] = m_new

    tile(i, True)

    def body(j, carry):
        tile(j, False)
        return carry

    lax.fori_loop(0, i, body, 0)

    a0 = acc_ref[0:blk, :]
    a1 = acc_ref[blk:2 * blk, :]
    l0 = a0[:, HEAD_DIM:HEAD_DIM + 1]
    l1 = a1[:, 0:1]
    o_ref[0] = jnp.where(head0, a0 / l0, a1 / l1).astype(o_ref.dtype)


def _attention(q, k, v):
    b, s, d = q.shape
    blk = MOBA_BLOCK
    nb = s // blk
    assert s % blk == 0 and nb <= HEAD_DIM, "block-select lanes hold at most 64 blocks"
    slopes = 2.0 ** (-8.0 * jnp.arange(1, N_HEADS + 1, dtype=F32) / N_HEADS)
    pairs = d // LANES
    full = pl.BlockSpec((1, s, LANES), lambda bi, pi, ii: (bi, 0, pi))
    tile = pl.BlockSpec((1, blk, LANES), lambda bi, pi, ii: (bi, ii, pi))
    return pl.pallas_call(
        functools.partial(_attn_kernel, nb=nb),
        grid=(b, pairs, nb),
        in_specs=[pl.BlockSpec(memory_space=pltpu.SMEM), tile, full, full],
        out_specs=tile,
        out_shape=jax.ShapeDtypeStruct((b, s, d), BF16),
        scratch_shapes=[pltpu.VMEM((LANES, LANES), F32),
                        pltpu.VMEM((s, 2 * LANES), BF16),
                        pltpu.VMEM((s, LANES), BF16),
                        pltpu.VMEM((s, LANES), BF16),
                        pltpu.VMEM((2 * blk, 2 * LANES), BF16),
                        pltpu.VMEM((2 * blk, 1), F32),
                        pltpu.VMEM((2 * blk, LANES), F32)],
        compiler_params=_cparams(3),
        name="moba_attn",
    )(slopes, q, k, v)


def _oproj_kernel(o_ref, x_ref, mod_ref, w_ref, out_ref):
    y = jnp.dot(o_ref[0], w_ref[...], preferred_element_type=F32)
    out_ref[0] = x_ref[0] + mod_ref[0][2:3] * y


def _oproj(o, x, mod, w_o):
    b, s, d = x.shape
    ts = 512
    blk = pl.BlockSpec((1, ts, d), lambda bi, si: (bi, si, 0))
    return pl.pallas_call(
        _oproj_kernel,
        grid=(b, s // ts),
        in_specs=[blk, blk,
                  pl.BlockSpec((1, 6, d), lambda bi, si: (bi, 0, 0)),
                  pl.BlockSpec((d, d), lambda bi, si: (0, 0))],
        out_specs=blk,
        out_shape=jax.ShapeDtypeStruct((b, s, d), F32),
        compiler_params=_cparams(2),
        name="attn_oproj",
    )(o, x, mod, w_o.astype(BF16))


def kernel(x, c, ada_w, ada_b, norm_mix_g, norm_ffn_g, pool_w, pool_scale, w_qkv, w_o,
           router_group_w, router_expert_w, exp_w_gate, exp_w_up, exp_w_down, final_norm_g):
    b, s, d = x.shape
    depth = ada_w.shape[0]
    mod = _adaln(c, ada_w, ada_b)
    for i in range(depth):
        if i % 2 == 0:
            x = _pool_layer(x, mod[i], norm_mix_g[i], pool_w[i // 2], pool_scale[i // 2])
        else:
            q, k, v = _qkv(x, mod[i], norm_mix_g[i], w_qkv[i // 2])
            o = _attention(q, k, v)
            x = _oproj(o, x, mod[i], w_o[i // 2])
        x = _moe_layer(x.reshape(b * s, d), mod[i], norm_ffn_g[i], router_group_w[i],
                       router_expert_w[i], exp_w_gate[i], exp_w_up[i], exp_w_down[i],
                       final_norm_g, s, final_norm=(i == depth - 1)).reshape(b, s, d)
    return x
```

```python
import functools

import jax
import jax.numpy as jnp
from jax import lax
from jax.experimental import pallas as pl
from jax.experimental.pallas import tpu as pltpu

F32 = jnp.float32
BF16 = jnp.bfloat16
HIGHEST = lax.Precision.HIGHEST

RMS_EPS = 1e-6
POOL_WINDOWS = (2, 4, 8, 16)
POOL_HALO = 16
N_HEADS = 16
HEAD_DIM = 64
MOBA_BLOCK = 256
MOBA_TOPK = 3
ATTN_GROUP = 4
N_EXPERT_GROUPS = 4
EXPERTS_PER_GROUP = 8
N_EXPERTS = N_EXPERT_GROUPS * EXPERTS_PER_GROUP
LANES = 128
NEG_BIG = -1e30
LOG2E = 1.4426950408889634
VMEM_LIMIT = 56 * 1024 * 1024


def _cparams(n_axes):
    return pltpu.CompilerParams(dimension_semantics=("arbitrary",) * n_axes,
                                vmem_limit_bytes=VMEM_LIMIT)


def _rms_mod(x, g, sc, sh):
    r = lax.rsqrt(jnp.mean(x * x, axis=-1, keepdims=True) + RMS_EPS)
    return (x * r) * g * (1.0 + sc) + sh


def _adaln_kernel(c_ref, w_ref, b_ref, o_ref):
    c = c_ref[...]
    ca = c / (1.0 + jnp.exp(-c))
    o_ref[0] = jnp.dot(ca, w_ref[0], precision=HIGHEST, preferred_element_type=F32) + b_ref[0]


def _adaln(c, ada_w, ada_b):
    depth, d, n = ada_w.shape
    b = c.shape[0]
    rows = 8
    cp = jnp.zeros((rows, d), F32).at[:b].set(c)
    tn = 1536
    out = pl.pallas_call(
        _adaln_kernel,
        grid=(depth, n // tn),
        in_specs=[pl.BlockSpec((rows, d), lambda i, j: (0, 0)),
                  pl.BlockSpec((1, d, tn), lambda i, j: (i, 0, j)),
                  pl.BlockSpec((1, 1, tn), lambda i, j: (i, 0, j))],
        out_specs=pl.BlockSpec((1, rows, tn), lambda i, j: (i, 0, j)),
        out_shape=jax.ShapeDtypeStruct((depth, rows, n), F32),
        compiler_params=_cparams(2),
        name="adaln",
    )(cp, ada_w, ada_b.reshape(depth, 1, n))
    return out[:, :b].reshape(depth, b, 6, d)


def _pool_kernel(x_ref, mod_ref, g_ref, w_ref, ps_ref, o_ref, hb_ref, *, ts):
    s = pl.program_id(1)

    @pl.when(s == 0)
    def _():
        hb_ref[0:POOL_HALO, :] = jnp.zeros((POOL_HALO, hb_ref.shape[1]), F32)

    x = x_ref[0]
    mod = mod_ref[0]
    h = _rms_mod(x, g_ref[...], mod[1:2], mod[0:1])
    hb_ref[POOL_HALO:POOL_HALO + ts, :] = h
    pos = s * ts + lax.broadcasted_iota(jnp.int32, (ts, 1), 0)
    gd = x.shape[1] // len(POOL_WINDOWS)
    ys = []
    for gi, w in enumerate(POOL_WINDOWS):
        c0 = gi * gd
        hg = h[:, c0:c0 + gd]
        acc = hg
        for k in range(1, w):
            acc = acc + hb_ref[POOL_HALO - k:POOL_HALO - k + ts, c0:c0 + gd]
        cnt = jnp.minimum(pos + 1, w).astype(F32)
        p = (acc / cnt - hg).astype(BF16)
        ys.append(jnp.dot(p, w_ref[gi], preferred_element_type=F32))
    y = jnp.concatenate(ys, axis=-1) * ps_ref[...]
    o_ref[0] = x + mod[2:3] * y
    hb_ref[0:POOL_HALO, :] = hb_ref[ts:ts + POOL_HALO, :]


def _pool_layer(x, mod, norm_g, pool_w, pool_scale):
    b, s, d = x.shape
    ts = 512
    g = len(POOL_WINDOWS)
    return pl.pallas_call(
        functools.partial(_pool_kernel, ts=ts),
        grid=(b, s // ts),
        in_specs=[pl.BlockSpec((1, ts, d), lambda bi, si: (bi, si, 0)),
                  pl.BlockSpec((1, 6, d), lambda bi, si: (bi, 0, 0)),
                  pl.BlockSpec((1, d), lambda bi, si: (0, 0)),
                  pl.BlockSpec((g, d // g, d // g), lambda bi, si: (0, 0, 0)),
                  pl.BlockSpec((1, d), lambda bi, si: (0, 0))],
        out_specs=pl.BlockSpec((1, ts, d), lambda bi, si: (bi, si, 0)),
        out_shape=jax.ShapeDtypeStruct((b, s, d), F32),
        scratch_shapes=[pltpu.VMEM((POOL_HALO + ts, d), F32)],
        compiler_params=_cparams(2),
        name="pool_mixer",
    )(x, mod, norm_g.reshape(1, d), pool_w.astype(BF16), pool_scale.reshape(1, d))


def _route(logits):
    lane = lax.broadcasted_iota(jnp.int32, logits.shape, 1)
    big = jnp.int32(1 << 20)
    is_g = (lane >= N_EXPERTS) & (lane < N_EXPERTS + N_EXPERT_GROUPS)
    glog = jnp.where(is_g, logits, -jnp.inf)
    gmax = jnp.max(glog, axis=-1, keepdims=True)
    gsel = jnp.min(jnp.where(glog == gmax, lane, big), axis=-1, keepdims=True) - N_EXPERTS
    gsum = jnp.sum(jnp.exp(glog - gmax), axis=-1, keepdims=True)
    g_w = 1.0 / gsum
    lo = gsel * EXPERTS_PER_GROUP
    in_grp = (lane >= lo) & (lane < lo + EXPERTS_PER_GROUP)
    elog = jnp.where(in_grp, logits, -jnp.inf)
    m1 = jnp.max(elog, axis=-1, keepdims=True)
    i1 = jnp.min(jnp.where(elog == m1, lane, big), axis=-1, keepdims=True)
    elog2 = jnp.where(lane == i1, -jnp.inf, elog)
    m2 = jnp.max(elog2, axis=-1, keepdims=True)
    i2 = jnp.min(jnp.where(elog2 == m2, lane, big), axis=-1, keepdims=True)
    d = jnp.exp(m2 - m1)
    w1 = g_w / (1.0 + d)
    w2 = g_w * d / (1.0 + d)
    return jnp.where(lane == i1, w1, 0.0) + jnp.where(lane == i2, w2, 0.0)


def _moe_kernel(x_ref, mod_ref, g_ref, wr_ref, wgu_ref, wd_ref, fg_ref, o_ref,
                h_ref, comb_ref, acc_ref, *, final_norm):
    e = pl.program_id(1)
    mod = mod_ref[0]

    @pl.when(e == 0)
    def _():
        h = _rms_mod(x_ref[...], g_ref[...], mod[4:5], mod[3:4])
        h_ref[...] = h.astype(BF16)
        logits = jnp.dot(h, wr_ref[...], precision=HIGHEST, preferred_element_type=F32)
        comb_ref[...] = _route(logits)
        acc_ref[...] = jnp.zeros_like(acc_ref)

    f = wd_ref.shape[1]
    gu = jnp.dot(h_ref[...], wgu_ref[0], preferred_element_type=F32)
    gt = gu[:, :f]
    act = (gt / (1.0 + jnp.exp(-gt))) * gu[:, f:]
    out = jnp.dot(act.astype(BF16), wd_ref[0], preferred_element_type=F32)
    comb = comb_ref[...]
    lane = lax.broadcasted_iota(jnp.int32, comb.shape, 1)
    ce = jnp.sum(jnp.where(lane == e, comb, 0.0), axis=-1, keepdims=True)
    acc_ref[...] += ce * out

    @pl.when(e == pl.num_programs(1) - 1)
    def _():
        xn = x_ref[...] + mod[5:6] * acc_ref[...]
        if final_norm:
            r = lax.rsqrt(jnp.mean(xn * xn, axis=-1, keepdims=True) + RMS_EPS)
            xn = (xn * r) * fg_ref[...]
        o_ref[...] = xn


def _moe_layer(x2, mod, norm_g, w_group, w_expert, w_gate, w_up, w_down, final_g, seq, final_norm):
    t, d = x2.shape
    e, _, f = w_gate.shape
    tm = 1024
    wr = jnp.zeros((d, LANES), F32).at[:, :e].set(w_expert).at[:, e:e + N_EXPERT_GROUPS].set(w_group)
    wgu = jnp.concatenate([w_gate, w_up], axis=-1).astype(BF16)
    wd = w_down.astype(BF16)
    return pl.pallas_call(
        functools.partial(_moe_kernel, final_norm=final_norm),
        grid=(t // tm, e),
        in_specs=[pl.BlockSpec((tm, d), lambda ti, ei: (ti, 0)),
                  pl.BlockSpec((1, 6, d), lambda ti, ei: (ti * tm // seq, 0, 0)),
                  pl.BlockSpec((1, d), lambda ti, ei: (0, 0)),
                  pl.BlockSpec((d, LANES), lambda ti, ei: (0, 0)),
                  pl.BlockSpec((1, d, 2 * f), lambda ti, ei: (ei, 0, 0)),
                  pl.BlockSpec((1, f, d), lambda ti, ei: (ei, 0, 0)),
                  pl.BlockSpec((1, d), lambda ti, ei: (0, 0))],
        out_specs=pl.BlockSpec((tm, d), lambda ti, ei: (ti, 0)),
        out_shape=jax.ShapeDtypeStruct((t, d), F32),
        scratch_shapes=[pltpu.VMEM((tm, d), BF16),
                        pltpu.VMEM((tm, LANES), F32),
                        pltpu.VMEM((tm, d), F32)],
        compiler_params=_cparams(2),
        name="moe_final" if final_norm else "moe",
    )(x2, mod, norm_g.reshape(1, d), wr, wgu, wd, final_g.reshape(1, d))


def _qkv_kernel(x_ref, mod_ref, g_ref, w_ref, q_ref, k_ref, v_ref):
    mod = mod_ref[0]
    h = _rms_mod(x_ref[0], g_ref[...], mod[1:2], mod[0:1]).astype(BF16)
    d = h.shape[1]
    q_ref[0] = jnp.dot(h, w_ref[:, 0:d], preferred_element_type=F32)
    k_ref[0] = jnp.dot(h, w_ref[:, d:2 * d], preferred_element_type=F32).astype(BF16)
    v_ref[0] = jnp.dot(h, w_ref[:, 2 * d:3 * d], preferred_element_type=F32).astype(BF16)


def _qkv(x, mod, norm_g, w_qkv):
    b, s, d = x.shape
    ts = 512
    blk = pl.BlockSpec((1, ts, d), lambda bi, si: (bi, si, 0))
    return pl.pallas_call(
        _qkv_kernel,
        grid=(b, s // ts),
        in_specs=[blk,
                  pl.BlockSpec((1, 6, d), lambda bi, si: (bi, 0, 0)),
                  pl.BlockSpec((1, d), lambda bi, si: (0, 0)),
                  pl.BlockSpec((d, 3 * d), lambda bi, si: (0, 0))],
        out_specs=[blk, blk, blk],
        out_shape=[jax.ShapeDtypeStruct((b, s, d), F32),
                   jax.ShapeDtypeStruct((b, s, d), BF16),
                   jax.ShapeDtypeStruct((b, s, d), BF16)],
        compiler_params=_cparams(2),
        name="qkv_proj",
    )(x, mod, norm_g.reshape(1, d), w_qkv.astype(BF16))


_SLOPE_LANE = 64
_BLKIDX_LANE = 67


def _attn_kernel(slopes_ref, q_ref, k_ref, v_ref, o_ref,
                 kmean_ref, kext_ref, v0_ref, v1_ref, lhs_ref, dlhs_ref, sa_ref, sb_ref, m_ref, acc_ref,
                 *, nb):
    p = pl.program_id(1)
    i = pl.program_id(2)
    blk = MOBA_BLOCK
    lane = lax.broadcasted_iota(jnp.int32, (blk, LANES), 1)
    head0 = lane < HEAD_DIM

    @pl.when(i == 0)
    def _():
        kmean_ref[...] = jnp.zeros_like(kmean_ref)
        row = lax.broadcasted_iota(jnp.int32, (blk, LANES), 0).astype(F32)
        in_slope = (lane >= _SLOPE_LANE) & (lane < _SLOPE_LANE + 3)
        in_blk = (lane >= _BLKIDX_LANE) & (lane < _BLKIDX_LANE + 3)
        ones0 = jnp.where(lane == HEAD_DIM, 1.0, 0.0)
        ones1 = jnp.where(lane == 0, 1.0, 0.0)

        def build(j, carry):
            rows = pl.ds(pl.multiple_of(j * blk, blk), blk)
            kj = k_ref[0, rows, :]
            kmean_ref[pl.ds(j, 1), :] = jnp.mean(kj.astype(F32), axis=0, keepdims=True)
            jf = j.astype(F32)
            extra = jnp.where(lane == j, 1.0, jnp.where(in_slope, row, jnp.where(in_blk, jf, 0.0)))
            kext_ref[rows, 0:LANES] = kj
            kext_ref[rows, LANES:2 * LANES] = extra.astype(BF16)
            vj = v_ref[0, rows, :].astype(F32)
            v0_ref[rows, :] = jnp.where(head0, vj, ones0).astype(BF16)
            v1_ref[rows, :] = jnp.where(head0, ones1, vj).astype(BF16)
            return carry

        lax.fori_loop(0, nb, build, 0)

    q = q_ref[0] * (HEAD_DIM ** -0.5 * LOG2E)
    kmean = kmean_ref[...]
    valid = lane < i
    big = jnp.int32(1 << 20)
    for hh in range(2):
        qm = jnp.where(head0 if hh == 0 else ~head0, q, 0.0)
        gate = lax.dot_general(qm, kmean, (((1,), (1,)), ((), ())),
                               precision=HIGHEST, preferred_element_type=F32)
        gate = jnp.where(valid, gate, -jnp.inf)
        sel = lane < 0
        for _ in range(MOBA_TOPK):
            mx = jnp.max(gate, axis=-1, keepdims=True)
            idx = jnp.min(jnp.where(gate == mx, lane, big), axis=-1, keepdims=True)
            hit = lane == idx
            sel = sel | (hit & valid)
            gate = jnp.where(hit, -jnp.inf, gate)
        slope = jnp.full((blk, LANES), slopes_ref[2 * p + hh] * LOG2E, F32)
        s_hi = slope.astype(BF16).astype(F32)
        rem = slope - s_hi
        s_mid = rem.astype(BF16).astype(F32)
        s_lo = rem - s_mid
        ext = jnp.where(sel, 0.0, NEG_BIG)
        for k, part in enumerate((s_hi, s_mid, s_lo)):
            ext = jnp.where(lane == _SLOPE_LANE + k, part, ext)
            ext = jnp.where(lane == _BLKIDX_LANE + k, part * float(blk), ext)
        ext = jnp.where(lane >= _BLKIDX_LANE + 3, 0.0, ext)
        r0 = hh * blk
        lhs_ref[r0:r0 + blk, 0:LANES] = qm.astype(BF16)
        lhs_ref[r0:r0 + blk, LANES:2 * LANES] = ext.astype(BF16)
        dlhs_ref[r0:r0 + blk, 0:LANES] = qm.astype(BF16)
        dlhs_ref[r0:r0 + blk, LANES:2 * LANES] = jnp.where(lane == i, 0.0, ext).astype(BF16)

    m_ref[...] = jnp.full(m_ref.shape, NEG_BIG, F32)
    acc_ref[...] = jnp.zeros_like(acc_ref)
    nt = (((1,), (1,)), ((), ()))
    gw = ATTN_GROUP * blk

    def softmax_pv(hh, s, rows):
        r = slice(hh * blk, (hh + 1) * blk)
        m_old = m_ref[r, :]
        m_new = jnp.maximum(m_old, jnp.max(s, axis=-1, keepdims=True))
        alpha = jnp.exp2(m_old - m_new)
        pm = jnp.exp2(s - m_new).astype(BF16)
        vref = v0_ref if hh == 0 else v1_ref
        pv = jnp.dot(pm, vref[rows, :], preferred_element_type=F32)
        acc_ref[r, :] = alpha * acc_ref[r, :] + pv
        m_ref[r, :] = m_new

    drows = pl.ds(pl.multiple_of(i * blk, blk), blk)
    sd = lax.dot_general(dlhs_ref[...], kext_ref[drows, :], nt, preferred_element_type=F32)
    qi = lax.broadcasted_iota(jnp.int32, sd.shape, 0) & (blk - 1)
    ki = lax.broadcasted_iota(jnp.int32, sd.shape, 1)
    sd = jnp.where(ki <= qi, sd, NEG_BIG)
    for hh in range(2):
        softmax_pv(hh, sd[hh * blk:(hh + 1) * blk], drows)

    npair = (i + 2 * ATTN_GROUP - 1) // (2 * ATTN_GROUP)

    def group_rows(g):
        return pl.ds(pl.multiple_of(g * gw, gw), gw)

    def scores(g, dst_ref):
        dst_ref[...] = lax.dot_general(lhs_ref[...], kext_ref[group_rows(g), :], nt,
                                       preferred_element_type=F32)

    def softmax_group(g, src_ref):
        for hh in range(2):
            softmax_pv(hh, src_ref[hh * blk:(hh + 1) * blk, :], group_rows(g))

    @pl.when(npair > 0)
    def _():
        scores(0, sa_ref)

    def body(t, carry):
        scores(2 * t + 1, sb_ref)
        softmax_group(2 * t, sa_ref)
        scores(jnp.minimum(2 * t + 2, 2 * npair - 2), sa_ref)
        softmax_group(2 * t + 1, sb_ref)
        return carry

    lax.fori_loop(0, npair, body, 0)

    a0 = acc_ref[0:blk, :]
    a1 = acc_ref[blk:2 * blk, :]
    l0 = a0[:, HEAD_DIM:HEAD_DIM + 1]
    l1 = a1[:, 0:1]
    o_ref[0] = jnp.where(head0, a0 / l0, a1 / l1).astype(o_ref.dtype)


def _attention(q, k, v):
    b, s, d = q.shape
    blk = MOBA_BLOCK
    nb = s // blk
    assert s % blk == 0 and nb <= HEAD_DIM, "block-select lanes hold at most 64 blocks"
    assert nb % (2 * ATTN_GROUP) == 0, "past blocks are scored in whole pairs of groups"
    slopes = 2.0 ** (-8.0 * jnp.arange(1, N_HEADS + 1, dtype=F32) / N_HEADS)
    pairs = d // LANES
    full = pl.BlockSpec((1, s, LANES), lambda bi, pi, ii: (bi, 0, pi))
    tile = pl.BlockSpec((1, blk, LANES), lambda bi, pi, ii: (bi, ii, pi))
    return pl.pallas_call(
        functools.partial(_attn_kernel, nb=nb),
        grid=(b, pairs, nb),
        in_specs=[pl.BlockSpec(memory_space=pltpu.SMEM), tile, full, full],
        out_specs=tile,
        out_shape=jax.ShapeDtypeStruct((b, s, d), BF16),
        scratch_shapes=[pltpu.VMEM((LANES, LANES), F32),
                        pltpu.VMEM((s, 2 * LANES), BF16),
                        pltpu.VMEM((s, LANES), BF16),
                        pltpu.VMEM((s, LANES), BF16),
                        pltpu.VMEM((2 * blk, 2 * LANES), BF16),
                        pltpu.VMEM((2 * blk, 2 * LANES), BF16),
                        pltpu.VMEM((2 * blk, ATTN_GROUP * blk), F32),
                        pltpu.VMEM((2 * blk, ATTN_GROUP * blk), F32),
                        pltpu.VMEM((2 * blk, 1), F32),
                        pltpu.VMEM((2 * blk, LANES), F32)],
        compiler_params=_cparams(3),
        name="moba_attn",
    )(slopes, q, k, v)


def _oproj_kernel(o_ref, x_ref, mod_ref, w_ref, out_ref):
    y = jnp.dot(o_ref[0], w_ref[...], preferred_element_type=F32)
    out_ref[0] = x_ref[0] + mod_ref[0][2:3] * y


def _oproj(o, x, mod, w_o):
    b, s, d = x.shape
    ts = 512
    blk = pl.BlockSpec((1, ts, d), lambda bi, si: (bi, si, 0))
    return pl.pallas_call(
        _oproj_kernel,
        grid=(b, s // ts),
        in_specs=[blk, blk,
                  pl.BlockSpec((1, 6, d), lambda bi, si: (bi, 0, 0)),
                  pl.BlockSpec((d, d), lambda bi, si: (0, 0))],
        out_specs=blk,
        out_shape=jax.ShapeDtypeStruct((b, s, d), F32),
        compiler_params=_cparams(2),
        name="attn_oproj",
    )(o, x, mod, w_o.astype(BF16))


def kernel(x, c, ada_w, ada_b, norm_mix_g, norm_ffn_g, pool_w, pool_scale, w_qkv, w_o,
           router_group_w, router_expert_w, exp_w_gate, exp_w_up, exp_w_down, final_norm_g):
    b, s, d = x.shape
    depth = ada_w.shape[0]
    mod = _adaln(c, ada_w, ada_b)
    for i in range(depth):
        if i % 2 == 0:
            x = _pool_layer(x, mod[i], norm_mix_g[i], pool_w[i // 2], pool_scale[i // 2])
        else:
            q, k, v = _qkv(x, mod[i], norm_mix_g[i], w_qkv[i // 2])
            o = _attention(q, k, v)
            x = _oproj(o, x, mod[i], w_o[i // 2])
        x = _moe_layer(x.reshape(b * s, d), mod[i], norm_ffn_g[i], router_group_w[i],
                       router_expert_w[i], exp_w_gate[i], exp_w_up[i], exp_w_down[i],
                       final_norm_g, s, final_norm=(i == depth - 1)).reshape(b, s, d)
    return x
```

```python
import functools

import jax
import jax.numpy as jnp
from jax import lax
from jax.experimental import pallas as pl
from jax.experimental.pallas import tpu as pltpu

F32 = jnp.float32
BF16 = jnp.bfloat16
HIGHEST = lax.Precision.HIGHEST

RMS_EPS = 1e-6
POOL_WINDOWS = (2, 4, 8, 16)
POOL_HALO = 16
N_HEADS = 16
HEAD_DIM = 64
MOBA_BLOCK = 256
MOBA_TOPK = 3
ATTN_GROUP = 4
N_EXPERT_GROUPS = 4
EXPERTS_PER_GROUP = 8
N_EXPERTS = N_EXPERT_GROUPS * EXPERTS_PER_GROUP
LANES = 128
NEG_BIG = -1e30
LOG2E = 1.4426950408889634
VMEM_LIMIT = 56 * 1024 * 1024


def _cparams(n_axes):
    return pltpu.CompilerParams(dimension_semantics=("arbitrary",) * n_axes,
                                vmem_limit_bytes=VMEM_LIMIT)


def _rms_mod(x, g, sc, sh):
    r = lax.rsqrt(jnp.mean(x * x, axis=-1, keepdims=True) + RMS_EPS)
    return (x * r) * g * (1.0 + sc) + sh


def _adaln_kernel(c_ref, w_ref, b_ref, o_ref):
    c = c_ref[...]
    ca = c / (1.0 + jnp.exp(-c))
    o_ref[0] = jnp.dot(ca, w_ref[0], precision=HIGHEST, preferred_element_type=F32) + b_ref[0]


def _adaln(c, ada_w, ada_b):
    depth, d, n = ada_w.shape
    b = c.shape[0]
    rows = 8
    cp = jnp.zeros((rows, d), F32).at[:b].set(c)
    tn = 1536
    out = pl.pallas_call(
        _adaln_kernel,
        grid=(depth, n // tn),
        in_specs=[pl.BlockSpec((rows, d), lambda i, j: (0, 0)),
                  pl.BlockSpec((1, d, tn), lambda i, j: (i, 0, j)),
                  pl.BlockSpec((1, 1, tn), lambda i, j: (i, 0, j))],
        out_specs=pl.BlockSpec((1, rows, tn), lambda i, j: (i, 0, j)),
        out_shape=jax.ShapeDtypeStruct((depth, rows, n), F32),
        compiler_params=_cparams(2),
        name="adaln",
    )(cp, ada_w, ada_b.reshape(depth, 1, n))
    return out[:, :b].reshape(depth, b, 6, d)


def _pool_kernel(x_ref, mod_ref, g_ref, w_ref, ps_ref, o_ref, hb_ref, *, ts):
    s = pl.program_id(1)

    @pl.when(s == 0)
    def _():
        hb_ref[0:POOL_HALO, :] = jnp.zeros((POOL_HALO, hb_ref.shape[1]), F32)

    x = x_ref[0]
    mod = mod_ref[0]
    h = _rms_mod(x, g_ref[...], mod[1:2], mod[0:1])
    hb_ref[POOL_HALO:POOL_HALO + ts, :] = h
    pos = s * ts + lax.broadcasted_iota(jnp.int32, (ts, 1), 0)
    gd = x.shape[1] // len(POOL_WINDOWS)
    ys = []
    for gi, w in enumerate(POOL_WINDOWS):
        c0 = gi * gd
        hg = h[:, c0:c0 + gd]
        acc = hg
        for k in range(1, w):
            acc = acc + hb_ref[POOL_HALO - k:POOL_HALO - k + ts, c0:c0 + gd]
        cnt = jnp.minimum(pos + 1, w).astype(F32)
        p = (acc / cnt - hg).astype(BF16)
        ys.append(jnp.dot(p, w_ref[gi], preferred_element_type=F32))
    y = jnp.concatenate(ys, axis=-1) * ps_ref[...]
    o_ref[0] = x + mod[2:3] * y
    hb_ref[0:POOL_HALO, :] = hb_ref[ts:ts + POOL_HALO, :]


def _pool_layer(x, mod, norm_g, pool_w, pool_scale):
    b, s, d = x.shape
    ts = 512
    g = len(POOL_WINDOWS)
    return pl.pallas_call(
        functools.partial(_pool_kernel, ts=ts),
        grid=(b, s // ts),
        in_specs=[pl.BlockSpec((1, ts, d), lambda bi, si: (bi, si, 0)),
                  pl.BlockSpec((1, 6, d), lambda bi, si: (bi, 0, 0)),
                  pl.BlockSpec((1, d), lambda bi, si: (0, 0)),
                  pl.BlockSpec((g, d // g, d // g), lambda bi, si: (0, 0, 0)),
                  pl.BlockSpec((1, d), lambda bi, si: (0, 0))],
        out_specs=pl.BlockSpec((1, ts, d), lambda bi, si: (bi, si, 0)),
        out_shape=jax.ShapeDtypeStruct((b, s, d), F32),
        scratch_shapes=[pltpu.VMEM((POOL_HALO + ts, d), F32)],
        compiler_params=_cparams(2),
        name="pool_mixer",
    )(x, mod, norm_g.reshape(1, d), pool_w.astype(BF16), pool_scale.reshape(1, d))


def _route(logits):
    lane = lax.broadcasted_iota(jnp.int32, logits.shape, 1)
    big = jnp.int32(1 << 20)
    is_g = (lane >= N_EXPERTS) & (lane < N_EXPERTS + N_EXPERT_GROUPS)
    glog = jnp.where(is_g, logits, -jnp.inf)
    gmax = jnp.max(glog, axis=-1, keepdims=True)
    gsel = jnp.min(jnp.where(glog == gmax, lane, big), axis=-1, keepdims=True) - N_EXPERTS
    gsum = jnp.sum(jnp.exp(glog - gmax), axis=-1, keepdims=True)
    g_w = 1.0 / gsum
    lo = gsel * EXPERTS_PER_GROUP
    in_grp = (lane >= lo) & (lane < lo + EXPERTS_PER_GROUP)
    elog = jnp.where(in_grp, logits, -jnp.inf)
    m1 = jnp.max(elog, axis=-1, keepdims=True)
    i1 = jnp.min(jnp.where(elog == m1, lane, big), axis=-1, keepdims=True)
    elog2 = jnp.where(lane == i1, -jnp.inf, elog)
    m2 = jnp.max(elog2, axis=-1, keepdims=True)
    i2 = jnp.min(jnp.where(elog2 == m2, lane, big), axis=-1, keepdims=True)
    d = jnp.exp(m2 - m1)
    w1 = g_w / (1.0 + d)
    w2 = g_w * d / (1.0 + d)
    return jnp.where(lane == i1, w1, 0.0) + jnp.where(lane == i2, w2, 0.0)


def _moe_kernel(x_ref, mod_ref, g_ref, wr_ref, wgu_ref, wd_ref, fg_ref, o_ref,
                h_ref, comb_ref, acc_ref, *, final_norm):
    e = pl.program_id(1)
    mod = mod_ref[0]

    @pl.when(e == 0)
    def _():
        h = _rms_mod(x_ref[...], g_ref[...], mod[4:5], mod[3:4])
        h_ref[...] = h.astype(BF16)
        logits = jnp.dot(h, wr_ref[...], precision=HIGHEST, preferred_element_type=F32)
        comb_ref[...] = _route(logits)
        acc_ref[...] = jnp.zeros_like(acc_ref)

    f = wd_ref.shape[1]
    gu = jnp.dot(h_ref[...], wgu_ref[0], preferred_element_type=F32)
    gt = gu[:, :f]
    act = (gt / (1.0 + jnp.exp(-gt))) * gu[:, f:]
    out = jnp.dot(act.astype(BF16), wd_ref[0], preferred_element_type=F32)
    comb = comb_ref[...]
    lane = lax.broadcasted_iota(jnp.int32, comb.shape, 1)
    ce = jnp.sum(jnp.where(lane == e, comb, 0.0), axis=-1, keepdims=True)
    acc_ref[...] += ce * out

    @pl.when(e == pl.num_programs(1) - 1)
    def _():
        xn = x_ref[...] + mod[5:6] * acc_ref[...]
        if final_norm:
            r = lax.rsqrt(jnp.mean(xn * xn, axis=-1, keepdims=True) + RMS_EPS)
            xn = (xn * r) * fg_ref[...]
        o_ref[...] = xn


def _moe_layer(x2, mod, norm_g, w_group, w_expert, w_gate, w_up, w_down, final_g, seq, final_norm):
    t, d = x2.shape
    e, _, f = w_gate.shape
    tm = 1024
    wr = jnp.zeros((d, LANES), F32).at[:, :e].set(w_expert).at[:, e:e + N_EXPERT_GROUPS].set(w_group)
    wgu = jnp.concatenate([w_gate, w_up], axis=-1).astype(BF16)
    wd = w_down.astype(BF16)
    return pl.pallas_call(
        functools.partial(_moe_kernel, final_norm=final_norm),
        grid=(t // tm, e),
        in_specs=[pl.BlockSpec((tm, d), lambda ti, ei: (ti, 0)),
                  pl.BlockSpec((1, 6, d), lambda ti, ei: (ti * tm // seq, 0, 0)),
                  pl.BlockSpec((1, d), lambda ti, ei: (0, 0)),
                  pl.BlockSpec((d, LANES), lambda ti, ei: (0, 0)),
                  pl.BlockSpec((1, d, 2 * f), lambda ti, ei: (ei, 0, 0)),
                  pl.BlockSpec((1, f, d), lambda ti, ei: (ei, 0, 0)),
                  pl.BlockSpec((1, d), lambda ti, ei: (0, 0))],
        out_specs=pl.BlockSpec((tm, d), lambda ti, ei: (ti, 0)),
        out_shape=jax.ShapeDtypeStruct((t, d), F32),
        scratch_shapes=[pltpu.VMEM((tm, d), BF16),
                        pltpu.VMEM((tm, LANES), F32),
                        pltpu.VMEM((tm, d), F32)],
        compiler_params=_cparams(2),
        name="moe_final" if final_norm else "moe",
    )(x2, mod, norm_g.reshape(1, d), wr, wgu, wd, final_g.reshape(1, d))


def _qkv_kernel(x_ref, mod_ref, g_ref, w_ref, q_ref, k_ref, v_ref):
    mod = mod_ref[0]
    h = _rms_mod(x_ref[0], g_ref[...], mod[1:2], mod[0:1]).astype(BF16)
    d = h.shape[1]
    q_ref[0] = jnp.dot(h, w_ref[:, 0:d], preferred_element_type=F32)
    k_ref[0] = jnp.dot(h, w_ref[:, d:2 * d], preferred_element_type=F32).astype(BF16)
    v_ref[0] = jnp.dot(h, w_ref[:, 2 * d:3 * d], preferred_element_type=F32).astype(BF16)


def _qkv(x, mod, norm_g, w_qkv):
    b, s, d = x.shape
    ts = 512
    blk = pl.BlockSpec((1, ts, d), lambda bi, si: (bi, si, 0))
    return pl.pallas_call(
        _qkv_kernel,
        grid=(b, s // ts),
        in_specs=[blk,
                  pl.BlockSpec((1, 6, d), lambda bi, si: (bi, 0, 0)),
                  pl.BlockSpec((1, d), lambda bi, si: (0, 0)),
                  pl.BlockSpec((d, 3 * d), lambda bi, si: (0, 0))],
        out_specs=[blk, blk, blk],
        out_shape=[jax.ShapeDtypeStruct((b, s, d), F32),
                   jax.ShapeDtypeStruct((b, s, d), BF16),
                   jax.ShapeDtypeStruct((b, s, d), BF16)],
        compiler_params=_cparams(2),
        name="qkv_proj",
    )(x, mod, norm_g.reshape(1, d), w_qkv.astype(BF16))


_SLOPE_LANE = 64
_BLKIDX_LANE = 67
_PAD_LANE = 70
PAD_BLOCKS = 2 * ATTN_GROUP
UNDERFLOW_LOG2 = -160.0
NORM_SLACK = 1.02


def _attn_kernel(slopes_ref, islopes_ref, q_ref, k_ref, v_ref, o_ref,
                 kmean_ref, knorm_ref, kext_ref, v0_ref, v1_ref, lhs_ref, dlhs_ref, sa_ref, sb_ref,
                 m_ref, acc_ref, *, nb):
    p = pl.program_id(1)
    i = pl.program_id(2)
    blk = MOBA_BLOCK
    pad = PAD_BLOCKS * blk
    lane = lax.broadcasted_iota(jnp.int32, (blk, LANES), 1)
    head0 = lane < HEAD_DIM

    @pl.when(i == 0)
    def _():
        kmean_ref[...] = jnp.zeros_like(kmean_ref)
        knorm_ref[...] = jnp.zeros_like(knorm_ref)
        kext_ref[0:pad, 0:LANES] = jnp.zeros((pad, LANES), BF16)
        lane_p = lax.broadcasted_iota(jnp.int32, (pad, LANES), 1)
        kext_ref[0:pad, LANES:2 * LANES] = jnp.where(lane_p == _PAD_LANE, 1.0, 0.0).astype(BF16)
        v0_ref[0:pad, :] = jnp.zeros((pad, LANES), BF16)
        v1_ref[0:pad, :] = jnp.zeros((pad, LANES), BF16)
        row = lax.broadcasted_iota(jnp.int32, (blk, LANES), 0).astype(F32)
        in_slope = (lane >= _SLOPE_LANE) & (lane < _SLOPE_LANE + 3)
        in_blk = (lane >= _BLKIDX_LANE) & (lane < _BLKIDX_LANE + 3)
        ones0 = jnp.where(lane == HEAD_DIM, 1.0, 0.0)
        ones1 = jnp.where(lane == 0, 1.0, 0.0)

        def build(j, carry):
            src = pl.ds(pl.multiple_of(j * blk, blk), blk)
            dst = pl.ds(pl.multiple_of(j * blk + pad, blk), blk)
            kj = k_ref[0, src, :]
            kf = kj.astype(F32)
            kmean_ref[pl.ds(j, 1), :] = jnp.mean(kf, axis=0, keepdims=True)
            ksq = kf * kf
            n0 = jnp.max(jnp.sum(jnp.where(head0, ksq, 0.0), axis=-1, keepdims=True),
                         axis=0, keepdims=True)
            n1 = jnp.max(jnp.sum(jnp.where(head0, 0.0, ksq), axis=-1, keepdims=True),
                         axis=0, keepdims=True)
            knorm_ref[0:1, :] = jnp.maximum(knorm_ref[0:1, :], n0)
            knorm_ref[1:2, :] = jnp.maximum(knorm_ref[1:2, :], n1)
            jf = jnp.full((blk, LANES), j, jnp.int32).astype(F32)
            extra = jnp.where(lane == j, 1.0, jnp.where(in_slope, row, jnp.where(in_blk, jf, 0.0)))
            kext_ref[dst, 0:LANES] = kj
            kext_ref[dst, LANES:2 * LANES] = extra.astype(BF16)
            vj = v_ref[0, src, :].astype(F32)
            v0_ref[dst, :] = jnp.where(head0, vj, ones0).astype(BF16)
            v1_ref[dst, :] = jnp.where(head0, ones1, vj).astype(BF16)
            return carry

        lax.fori_loop(0, nb, build, 0)

    nt = (((1,), (1,)), ((), ()))
    gw = ATTN_GROUP * blk
    q = q_ref[0] * (HEAD_DIM ** -0.5 * LOG2E)
    qs = jnp.concatenate([jnp.where(head0, q, 0.0), jnp.where(head0, 0.0, q)], axis=0)
    lane2 = lax.broadcasted_iota(jnp.int32, qs.shape, 1)
    lanef = lane2.astype(F32)
    top_half = lax.broadcasted_iota(jnp.int32, qs.shape, 0) < blk
    gate = lax.dot_general(qs, kmean_ref[...], nt, precision=HIGHEST, preferred_element_type=F32)
    valid = lane2 < i
    gate = jnp.where(valid, gate, -jnp.inf)
    sel = lane2 < 0
    for _ in range(MOBA_TOPK):
        mx = jnp.max(gate, axis=-1, keepdims=True)
        idx = jnp.min(jnp.where(gate == mx, lanef, 1e9), axis=-1, keepdims=True)
        hit = lanef == idx
        sel = sel | (hit & valid)
        gate = jnp.where(hit, -jnp.inf, gate)
    slope = jnp.where(top_half, slopes_ref[2 * p], slopes_ref[2 * p + 1]) * LOG2E
    s_hi = slope.astype(BF16).astype(F32)
    rem = slope - s_hi
    s_mid = rem.astype(BF16).astype(F32)
    s_lo = rem - s_mid
    ext = jnp.where(sel, 0.0, NEG_BIG)
    for k, part in enumerate((s_hi, s_mid, s_lo)):
        ext = jnp.where(lane2 == _SLOPE_LANE + k, part, ext)
        ext = jnp.where(lane2 == _BLKIDX_LANE + k, part * float(blk), ext)
    ext = jnp.where(lane2 > _PAD_LANE, 0.0, ext)
    qb = qs.astype(BF16)
    lhs_ref[:, 0:LANES] = qb
    lhs_ref[:, LANES:2 * LANES] = ext.astype(BF16)
    dlhs_ref[:, 0:LANES] = qb
    dlhs_ref[:, LANES:2 * LANES] = jnp.where(lane2 == i, 0.0, ext).astype(BF16)
    qnorm = jnp.sqrt(jnp.sum(qs * qs, axis=-1, keepdims=True))

    m_ref[...] = jnp.full(m_ref.shape, NEG_BIG, F32)
    acc_ref[...] = jnp.zeros_like(acc_ref)

    def softmax_pv(hh, s, rows):
        r = slice(hh * blk, (hh + 1) * blk)
        m_old = m_ref[r, :]
        m_new = jnp.maximum(m_old, jnp.max(s, axis=-1, keepdims=True))
        alpha = jnp.exp2(m_old - m_new)
        pm = jnp.exp2(s - m_new).astype(BF16)
        vref = v0_ref if hh == 0 else v1_ref
        pv = jnp.dot(pm, vref[rows, :], preferred_element_type=F32)
        acc_ref[r, :] = alpha * acc_ref[r, :] + pv
        m_ref[r, :] = m_new

    def group_rows(t):
        start = (i - ATTN_GROUP * (t + 1) + PAD_BLOCKS) * blk
        return pl.ds(pl.multiple_of(start, blk), gw)

    def scores(t, dst_ref):
        dst_ref[...] = lax.dot_general(lhs_ref[...], kext_ref[group_rows(t), :], nt,
                                       preferred_element_type=F32)

    def softmax_group(t, src_ref):
        for hh in range(2):
            softmax_pv(hh, src_ref[hh * blk:(hh + 1) * blk, :], group_rows(t))

    scores(0, sa_ref)

    drows = pl.ds(pl.multiple_of((i + PAD_BLOCKS) * blk, blk), blk)
    sd = lax.dot_general(dlhs_ref[...], kext_ref[drows, :], nt, preferred_element_type=F32)
    qi = lax.broadcasted_iota(jnp.int32, sd.shape, 0) & (blk - 1)
    ki = lax.broadcasted_iota(jnp.int32, sd.shape, 1)
    sd = jnp.where(ki <= qi, sd, NEG_BIG)
    for hh in range(2):
        softmax_pv(hh, sd[hh * blk:(hh + 1) * blk], drows)

    knorm = jnp.sqrt(knorm_ref[0:2, 0:1])
    excess = NORM_SLACK * qnorm * jnp.where(top_half[:, 0:1], knorm[0:1], knorm[1:2]) - m_ref[...]
    a0 = jnp.max(excess[0:blk], axis=0, keepdims=True)
    a1 = jnp.max(excess[blk:2 * blk], axis=0, keepdims=True)
    edge = (blk - 1.0) / blk
    x0 = (UNDERFLOW_LOG2 - a0) * islopes_ref[2 * p] - edge
    x1 = (UNDERFLOW_LOG2 - a1) * islopes_ref[2 * p + 1] - edge
    i_f = jnp.full((1, 1), i, jnp.int32).astype(F32)
    first_needed = jnp.clip(jnp.floor(jnp.minimum(x0, x1)), 0.0, i_f)
    ngroups = jnp.ceil((i_f - first_needed) * (1.0 / ATTN_GROUP)).astype(jnp.int32)[0, 0]
    npair = (ngroups + 1) // 2

    def body(t, carry):
        scores(2 * t + 1, sb_ref)
        softmax_group(2 * t, sa_ref)
        scores(jnp.minimum(2 * t + 2, 2 * npair - 2), sa_ref)
        softmax_group(2 * t + 1, sb_ref)
        return carry

    lax.fori_loop(0, npair, body, 0)

    a0 = acc_ref[0:blk, :]
    a1 = acc_ref[blk:2 * blk, :]
    l0 = a0[:, HEAD_DIM:HEAD_DIM + 1]
    l1 = a1[:, 0:1]
    o_ref[0] = jnp.where(head0, a0 / l0, a1 / l1).astype(o_ref.dtype)


def _attention(q, k, v):
    b, s, d = q.shape
    blk = MOBA_BLOCK
    nb = s // blk
    assert s % blk == 0 and nb <= HEAD_DIM, "block-select lanes hold at most 64 blocks"
    slopes = 2.0 ** (-8.0 * jnp.arange(1, N_HEADS + 1, dtype=F32) / N_HEADS)
    islopes = 1.0 / (slopes * (LOG2E * blk))
    pairs = d // LANES
    sp = s + PAD_BLOCKS * blk
    full = pl.BlockSpec((1, s, LANES), lambda bi, pi, ii: (bi, 0, pi))
    tile = pl.BlockSpec((1, blk, LANES), lambda bi, pi, ii: (bi, ii, pi))
    smem = pl.BlockSpec(memory_space=pltpu.SMEM)
    return pl.pallas_call(
        functools.partial(_attn_kernel, nb=nb),
        grid=(b, pairs, nb),
        in_specs=[smem, smem, tile, full, full],
        out_specs=tile,
        out_shape=jax.ShapeDtypeStruct((b, s, d), BF16),
        scratch_shapes=[pltpu.VMEM((LANES, LANES), F32),
                        pltpu.VMEM((8, LANES), F32),
                        pltpu.VMEM((sp, 2 * LANES), BF16),
                        pltpu.VMEM((sp, LANES), BF16),
                        pltpu.VMEM((sp, LANES), BF16),
                        pltpu.VMEM((2 * blk, 2 * LANES), BF16),
                        pltpu.VMEM((2 * blk, 2 * LANES), BF16),
                        pltpu.VMEM((2 * blk, ATTN_GROUP * blk), F32),
                        pltpu.VMEM((2 * blk, ATTN_GROUP * blk), F32),
                        pltpu.VMEM((2 * blk, 1), F32),
                        pltpu.VMEM((2 * blk, LANES), F32)],
        compiler_params=_cparams(3),
        name="moba_attn",
    )(slopes, islopes, q, k, v)


def _oproj_kernel(o_ref, x_ref, mod_ref, w_ref, out_ref):
    y = jnp.dot(o_ref[0], w_ref[...], preferred_element_type=F32)
    out_ref[0] = x_ref[0] + mod_ref[0][2:3] * y


def _oproj(o, x, mod, w_o):
    b, s, d = x.shape
    ts = 512
    blk = pl.BlockSpec((1, ts, d), lambda bi, si: (bi, si, 0))
    return pl.pallas_call(
        _oproj_kernel,
        grid=(b, s // ts),
        in_specs=[blk, blk,
                  pl.BlockSpec((1, 6, d), lambda bi, si: (bi, 0, 0)),
                  pl.BlockSpec((d, d), lambda bi, si: (0, 0))],
        out_specs=blk,
        out_shape=jax.ShapeDtypeStruct((b, s, d), F32),
        compiler_params=_cparams(2),
        name="attn_oproj",
    )(o, x, mod, w_o.astype(BF16))


def kernel(x, c, ada_w, ada_b, norm_mix_g, norm_ffn_g, pool_w, pool_scale, w_qkv, w_o,
           router_group_w, router_expert_w, exp_w_gate, exp_w_up, exp_w_down, final_norm_g):
    b, s, d = x.shape
    depth = ada_w.shape[0]
    mod = _adaln(c, ada_w, ada_b)
    for i in range(depth):
        if i % 2 == 0:
            x = _pool_layer(x, mod[i], norm_mix_g[i], pool_w[i // 2], pool_scale[i // 2])
        else:
            q, k, v = _qkv(x, mod[i], norm_mix_g[i], w_qkv[i // 2])
            o = _attention(q, k, v)
            x = _oproj(o, x, mod[i], w_o[i // 2])
        x = _moe_layer(x.reshape(b * s, d), mod[i], norm_ffn_g[i], router_group_w[i],
                       router_expert_w[i], exp_w_gate[i], exp_w_up[i], exp_w_down[i],
                       final_norm_g, s, final_norm=(i == depth - 1)).reshape(b, s, d)
    return x
```

```python
import functools

import jax
import jax.numpy as jnp
from jax import lax
from jax.experimental import pallas as pl
from jax.experimental.pallas import tpu as pltpu

F32 = jnp.float32
BF16 = jnp.bfloat16
HIGHEST = lax.Precision.HIGHEST

RMS_EPS = 1e-6
POOL_WINDOWS = (2, 4, 8, 16)
POOL_HALO = 16
N_HEADS = 16
HEAD_DIM = 64
MOBA_BLOCK = 256
MOBA_TOPK = 3
ATTN_GROUP = 4
N_EXPERT_GROUPS = 4
EXPERTS_PER_GROUP = 8
N_EXPERTS = N_EXPERT_GROUPS * EXPERTS_PER_GROUP
LANES = 128
NEG_BIG = -1e30
LOG2E = 1.4426950408889634
VMEM_LIMIT = 56 * 1024 * 1024


def _cparams(n_axes):
    return pltpu.CompilerParams(dimension_semantics=("arbitrary",) * n_axes,
                                vmem_limit_bytes=VMEM_LIMIT)


def _rms_mod(x, g, sc, sh):
    r = lax.rsqrt(jnp.mean(x * x, axis=-1, keepdims=True) + RMS_EPS)
    return (x * r) * g * (1.0 + sc) + sh


def _adaln_kernel(c_ref, w_ref, b_ref, o_ref):
    c = c_ref[...]
    ca = c / (1.0 + jnp.exp(-c))
    o_ref[0] = jnp.dot(ca, w_ref[0], precision=HIGHEST, preferred_element_type=F32) + b_ref[0]


def _adaln(c, ada_w, ada_b):
    depth, d, n = ada_w.shape
    b = c.shape[0]
    rows = 8
    cp = jnp.zeros((rows, d), F32).at[:b].set(c)
    tn = 1536
    out = pl.pallas_call(
        _adaln_kernel,
        grid=(depth, n // tn),
        in_specs=[pl.BlockSpec((rows, d), lambda i, j: (0, 0)),
                  pl.BlockSpec((1, d, tn), lambda i, j: (i, 0, j)),
                  pl.BlockSpec((1, 1, tn), lambda i, j: (i, 0, j))],
        out_specs=pl.BlockSpec((1, rows, tn), lambda i, j: (i, 0, j)),
        out_shape=jax.ShapeDtypeStruct((depth, rows, n), F32),
        compiler_params=_cparams(2),
        name="adaln",
    )(cp, ada_w, ada_b.reshape(depth, 1, n))
    return out[:, :b].reshape(depth, b, 6, d)


def _pool_kernel(x_ref, mod_ref, g_ref, w_ref, ps_ref, o_ref, hb_ref, *, ts):
    s = pl.program_id(1)

    @pl.when(s == 0)
    def _():
        hb_ref[0:POOL_HALO, :] = jnp.zeros((POOL_HALO, hb_ref.shape[1]), F32)

    x = x_ref[0]
    mod = mod_ref[0]
    h = _rms_mod(x, g_ref[...], mod[1:2], mod[0:1])
    hb_ref[POOL_HALO:POOL_HALO + ts, :] = h
    pos = s * ts + lax.broadcasted_iota(jnp.int32, (ts, 1), 0)
    gd = x.shape[1] // len(POOL_WINDOWS)
    ys = []
    for gi, w in enumerate(POOL_WINDOWS):
        c0 = gi * gd
        hg = h[:, c0:c0 + gd]
        acc = hg
        for k in range(1, w):
            acc = acc + hb_ref[POOL_HALO - k:POOL_HALO - k + ts, c0:c0 + gd]
        cnt = jnp.minimum(pos + 1, w).astype(F32)
        p = (acc / cnt - hg).astype(BF16)
        ys.append(jnp.dot(p, w_ref[gi], preferred_element_type=F32))
    y = jnp.concatenate(ys, axis=-1) * ps_ref[...]
    o_ref[0] = x + mod[2:3] * y
    hb_ref[0:POOL_HALO, :] = hb_ref[ts:ts + POOL_HALO, :]


def _pool_layer(x, mod, norm_g, pool_w, pool_scale):
    b, s, d = x.shape
    ts = 512
    g = len(POOL_WINDOWS)
    return pl.pallas_call(
        functools.partial(_pool_kernel, ts=ts),
        grid=(b, s // ts),
        in_specs=[pl.BlockSpec((1, ts, d), lambda bi, si: (bi, si, 0)),
                  pl.BlockSpec((1, 6, d), lambda bi, si: (bi, 0, 0)),
                  pl.BlockSpec((1, d), lambda bi, si: (0, 0)),
                  pl.BlockSpec((g, d // g, d // g), lambda bi, si: (0, 0, 0)),
                  pl.BlockSpec((1, d), lambda bi, si: (0, 0))],
        out_specs=pl.BlockSpec((1, ts, d), lambda bi, si: (bi, si, 0)),
        out_shape=jax.ShapeDtypeStruct((b, s, d), F32),
        scratch_shapes=[pltpu.VMEM((POOL_HALO + ts, d), F32)],
        compiler_params=_cparams(2),
        name="pool_mixer",
    )(x, mod, norm_g.reshape(1, d), pool_w.astype(BF16), pool_scale.reshape(1, d))


def _route(logits):
    lane = lax.broadcasted_iota(jnp.int32, logits.shape, 1)
    big = jnp.int32(1 << 20)
    is_g = (lane >= N_EXPERTS) & (lane < N_EXPERTS + N_EXPERT_GROUPS)
    glog = jnp.where(is_g, logits, -jnp.inf)
    gmax = jnp.max(glog, axis=-1, keepdims=True)
    gsel = jnp.min(jnp.where(glog == gmax, lane, big), axis=-1, keepdims=True) - N_EXPERTS
    gsum = jnp.sum(jnp.exp(glog - gmax), axis=-1, keepdims=True)
    g_w = 1.0 / gsum
    lo = gsel * EXPERTS_PER_GROUP
    in_grp = (lane >= lo) & (lane < lo + EXPERTS_PER_GROUP)
    elog = jnp.where(in_grp, logits, -jnp.inf)
    m1 = jnp.max(elog, axis=-1, keepdims=True)
    i1 = jnp.min(jnp.where(elog == m1, lane, big), axis=-1, keepdims=True)
    elog2 = jnp.where(lane == i1, -jnp.inf, elog)
    m2 = jnp.max(elog2, axis=-1, keepdims=True)
    i2 = jnp.min(jnp.where(elog2 == m2, lane, big), axis=-1, keepdims=True)
    d = jnp.exp(m2 - m1)
    w1 = g_w / (1.0 + d)
    w2 = g_w * d / (1.0 + d)
    return jnp.where(lane == i1, w1, 0.0) + jnp.where(lane == i2, w2, 0.0)


def _moe_kernel(x_ref, mod_ref, g_ref, wr_ref, wgu_ref, wd_ref, fg_ref, o_ref,
                h_ref, comb_ref, acc_ref, *, final_norm):
    e = pl.program_id(1)
    mod = mod_ref[0]

    @pl.when(e == 0)
    def _():
        h = _rms_mod(x_ref[...], g_ref[...], mod[4:5], mod[3:4])
        h_ref[...] = h.astype(BF16)
        logits = jnp.dot(h, wr_ref[...], precision=HIGHEST, preferred_element_type=F32)
        comb_ref[...] = _route(logits)
        acc_ref[...] = jnp.zeros_like(acc_ref)

    f = wd_ref.shape[1]
    gu = jnp.dot(h_ref[...], wgu_ref[0], preferred_element_type=F32)
    gt = gu[:, :f]
    act = (gt / (1.0 + jnp.exp(-gt))) * gu[:, f:]
    out = jnp.dot(act.astype(BF16), wd_ref[0], preferred_element_type=F32)
    comb = comb_ref[...]
    lane = lax.broadcasted_iota(jnp.int32, comb.shape, 1)
    ce = jnp.sum(jnp.where(lane == e, comb, 0.0), axis=-1, keepdims=True)
    acc_ref[...] += ce * out

    @pl.when(e == pl.num_programs(1) - 1)
    def _():
        xn = x_ref[...] + mod[5:6] * acc_ref[...]
        if final_norm:
            r = lax.rsqrt(jnp.mean(xn * xn, axis=-1, keepdims=True) + RMS_EPS)
            xn = (xn * r) * fg_ref[...]
        o_ref[...] = xn


def _moe_layer(x2, mod, norm_g, w_group, w_expert, w_gate, w_up, w_down, final_g, seq, final_norm):
    t, d = x2.shape
    e, _, f = w_gate.shape
    tm = 1024
    wr = jnp.zeros((d, LANES), F32).at[:, :e].set(w_expert).at[:, e:e + N_EXPERT_GROUPS].set(w_group)
    wgu = jnp.concatenate([w_gate, w_up], axis=-1).astype(BF16)
    wd = w_down.astype(BF16)
    return pl.pallas_call(
        functools.partial(_moe_kernel, final_norm=final_norm),
        grid=(t // tm, e),
        in_specs=[pl.BlockSpec((tm, d), lambda ti, ei: (ti, 0)),
                  pl.BlockSpec((1, 6, d), lambda ti, ei: (ti * tm // seq, 0, 0)),
                  pl.BlockSpec((1, d), lambda ti, ei: (0, 0)),
                  pl.BlockSpec((d, LANES), lambda ti, ei: (0, 0)),
                  pl.BlockSpec((1, d, 2 * f), lambda ti, ei: (ei, 0, 0)),
                  pl.BlockSpec((1, f, d), lambda ti, ei: (ei, 0, 0)),
                  pl.BlockSpec((1, d), lambda ti, ei: (0, 0))],
        out_specs=pl.BlockSpec((tm, d), lambda ti, ei: (ti, 0)),
        out_shape=jax.ShapeDtypeStruct((t, d), F32),
        scratch_shapes=[pltpu.VMEM((tm, d), BF16),
                        pltpu.VMEM((tm, LANES), F32),
                        pltpu.VMEM((tm, d), F32)],
        compiler_params=_cparams(2),
        name="moe_final" if final_norm else "moe",
    )(x2, mod, norm_g.reshape(1, d), wr, wgu, wd, final_g.reshape(1, d))


def _qkv_kernel(x_ref, mod_ref, g_ref, w_ref, q_ref, k_ref, v_ref):
    mod = mod_ref[0]
    h = _rms_mod(x_ref[0], g_ref[...], mod[1:2], mod[0:1]).astype(BF16)
    d = h.shape[1]
    q_ref[0] = jnp.dot(h, w_ref[:, 0:d], preferred_element_type=F32)
    k_ref[0] = jnp.dot(h, w_ref[:, d:2 * d], preferred_element_type=F32).astype(BF16)
    v_ref[0] = jnp.dot(h, w_ref[:, 2 * d:3 * d], preferred_element_type=F32).astype(BF16)


def _qkv(x, mod, norm_g, w_qkv):
    b, s, d = x.shape
    ts = 512
    blk = pl.BlockSpec((1, ts, d), lambda bi, si: (bi, si, 0))
    return pl.pallas_call(
        _qkv_kernel,
        grid=(b, s // ts),
        in_specs=[blk,
                  pl.BlockSpec((1, 6, d), lambda bi, si: (bi, 0, 0)),
                  pl.BlockSpec((1, d), lambda bi, si: (0, 0)),
                  pl.BlockSpec((d, 3 * d), lambda bi, si: (0, 0))],
        out_specs=[blk, blk, blk],
        out_shape=[jax.ShapeDtypeStruct((b, s, d), F32),
                   jax.ShapeDtypeStruct((b, s, d), BF16),
                   jax.ShapeDtypeStruct((b, s, d), BF16)],
        compiler_params=_cparams(2),
        name="qkv_proj",
    )(x, mod, norm_g.reshape(1, d), w_qkv.astype(BF16))


_SLOPE_LANE = 64
_BLKIDX_LANE = 67
_PAD_LANE = 70
PAD_BLOCKS = 2 * ATTN_GROUP
UNDERFLOW_LOG2 = -160.0
NORM_SLACK = 1.02


def _attn_kernel(slopes_ref, islopes_ref, q_ref, k_ref, v_ref, o_ref,
                 kmean_ref, knorm_ref, kext_ref, v0_ref, v1_ref, lhs_ref, sa_ref, sb_ref,
                 m_ref, acc_ref, *, nb):
    p = pl.program_id(1)
    i = pl.program_id(2)
    blk = MOBA_BLOCK
    pad = PAD_BLOCKS * blk
    lane = lax.broadcasted_iota(jnp.int32, (blk, LANES), 1)
    head0 = lane < HEAD_DIM

    @pl.when(i == 0)
    def _():
        kmean_ref[...] = jnp.zeros_like(kmean_ref)
        knorm_ref[...] = jnp.zeros_like(knorm_ref)
        kext_ref[0:pad, 0:LANES] = jnp.zeros((pad, LANES), BF16)
        lane_p = lax.broadcasted_iota(jnp.int32, (pad, LANES), 1)
        kext_ref[0:pad, LANES:2 * LANES] = jnp.where(lane_p == _PAD_LANE, 1.0, 0.0).astype(BF16)
        v0_ref[0:pad, :] = jnp.zeros((pad, LANES), BF16)
        v1_ref[0:pad, :] = jnp.zeros((pad, LANES), BF16)
        row = lax.broadcasted_iota(jnp.int32, (blk, LANES), 0).astype(F32)
        in_slope = (lane >= _SLOPE_LANE) & (lane < _SLOPE_LANE + 3)
        in_blk = (lane >= _BLKIDX_LANE) & (lane < _BLKIDX_LANE + 3)
        ones0 = jnp.where(lane == HEAD_DIM, 1.0, 0.0)
        ones1 = jnp.where(lane == 0, 1.0, 0.0)

        def build(j, carry):
            src = pl.ds(pl.multiple_of(j * blk, blk), blk)
            dst = pl.ds(pl.multiple_of(j * blk + pad, blk), blk)
            kj = k_ref[0, src, :]
            kf = kj.astype(F32)
            kmean_ref[pl.ds(j, 1), :] = jnp.mean(kf, axis=0, keepdims=True)
            ksq = kf * kf
            n0 = jnp.max(jnp.sum(jnp.where(head0, ksq, 0.0), axis=-1, keepdims=True),
                         axis=0, keepdims=True)
            n1 = jnp.max(jnp.sum(jnp.where(head0, 0.0, ksq), axis=-1, keepdims=True),
                         axis=0, keepdims=True)
            knorm_ref[0:1, :] = jnp.maximum(knorm_ref[0:1, :], n0)
            knorm_ref[1:2, :] = jnp.maximum(knorm_ref[1:2, :], n1)
            jf = jnp.full((blk, LANES), j, jnp.int32).astype(F32)
            extra = jnp.where(lane == j, 1.0, jnp.where(in_slope, row, jnp.where(in_blk, jf, 0.0)))
            kext_ref[dst, 0:LANES] = kj
            kext_ref[dst, LANES:2 * LANES] = extra.astype(BF16)
            vj = v_ref[0, src, :].astype(F32)
            v0_ref[dst, :] = jnp.where(head0, vj, ones0).astype(BF16)
            v1_ref[dst, :] = jnp.where(head0, ones1, vj).astype(BF16)
            return carry

        lax.fori_loop(0, nb, build, 0)

    nt = (((1,), (1,)), ((), ()))
    gw = ATTN_GROUP * blk
    q = q_ref[0] * (HEAD_DIM ** -0.5 * LOG2E)
    qs = jnp.concatenate([jnp.where(head0, q, 0.0), jnp.where(head0, 0.0, q)], axis=0)
    lane2 = lax.broadcasted_iota(jnp.int32, qs.shape, 1)
    top_half = lax.broadcasted_iota(jnp.int32, qs.shape, 0) < blk
    gate = lax.dot_general(kmean_ref[...], qs, nt, precision=HIGHEST, preferred_element_type=F32)
    blk_i = lax.broadcasted_iota(jnp.int32, gate.shape, 0)
    blk_f = blk_i.astype(F32)
    valid = blk_i < i
    gate = jnp.where(valid, gate, -jnp.inf)
    sel = blk_i == i
    for _ in range(MOBA_TOPK):
        mx = jnp.max(gate, axis=0, keepdims=True)
        idx = jnp.min(jnp.where(gate == mx, blk_f, 1e9), axis=0, keepdims=True)
        hit = blk_f == idx
        sel = sel | (hit & valid)
        gate = jnp.where(hit, -jnp.inf, gate)
    ext = jnp.where(sel, 0.0, NEG_BIG).T
    slope = jnp.where(top_half, slopes_ref[2 * p], slopes_ref[2 * p + 1]) * LOG2E
    s_hi = slope.astype(BF16).astype(F32)
    rem = slope - s_hi
    s_mid = rem.astype(BF16).astype(F32)
    s_lo = rem - s_mid
    for k, part in enumerate((s_hi, s_mid, s_lo)):
        ext = jnp.where(lane2 == _SLOPE_LANE + k, part, ext)
        ext = jnp.where(lane2 == _BLKIDX_LANE + k, part * float(blk), ext)
    ext = jnp.where(lane2 > _PAD_LANE, 0.0, ext)
    lhs_ref[:, 0:LANES] = qs.astype(BF16)
    lhs_ref[:, LANES:2 * LANES] = ext.astype(BF16)
    qnorm = jnp.sqrt(jnp.sum(qs * qs, axis=-1, keepdims=True))

    m_ref[...] = jnp.full(m_ref.shape, NEG_BIG, F32)
    acc_ref[...] = jnp.zeros_like(acc_ref)

    def group_rows(t):
        start = (i - ATTN_GROUP * (t + 1) + 1 + PAD_BLOCKS) * blk
        return pl.ds(pl.multiple_of(start, blk), gw)

    def scores(t, dst_ref):
        dst_ref[...] = lax.dot_general(lhs_ref[...], kext_ref[group_rows(t), :], nt,
                                       preferred_element_type=F32)

    def softmax_group(t, src_ref, own_block=False):
        rows = group_rows(t)
        for hh in range(2):
            r = slice(hh * blk, (hh + 1) * blk)
            s = src_ref[r, :]
            if own_block:
                qi = lax.broadcasted_iota(jnp.int32, s.shape, 0)
                ki = lax.broadcasted_iota(jnp.int32, s.shape, 1) - (gw - blk)
                s = jnp.where(ki <= qi, s, NEG_BIG)
            m_old = m_ref[r, :]
            m_new = jnp.maximum(m_old, jnp.max(s, axis=-1, keepdims=True))
            alpha = jnp.exp2(m_old - m_new)
            pm = jnp.exp2(s - m_new).astype(BF16)
            vref = v0_ref if hh == 0 else v1_ref
            pv = jnp.dot(pm, vref[rows, :], preferred_element_type=F32)
            acc_ref[r, :] = alpha * acc_ref[r, :] + pv
            m_ref[r, :] = m_new

    scores(0, sa_ref)
    scores(1, sb_ref)
    softmax_group(0, sa_ref, own_block=True)

    knorm = jnp.sqrt(knorm_ref[0:2, 0:1])
    excess = NORM_SLACK * qnorm * jnp.where(top_half[:, 0:1], knorm[0:1], knorm[1:2]) - m_ref[...]
    a0 = jnp.max(excess[0:blk], axis=0, keepdims=True)
    a1 = jnp.max(excess[blk:2 * blk], axis=0, keepdims=True)
    edge = (blk - 1.0) / blk
    x0 = (UNDERFLOW_LOG2 - a0) * islopes_ref[2 * p] - edge
    x1 = (UNDERFLOW_LOG2 - a1) * islopes_ref[2 * p + 1] - edge
    i_f = jnp.full((1, 1), i, jnp.int32).astype(F32)
    first_needed = jnp.clip(jnp.floor(jnp.minimum(x0, x1)), 0.0, i_f)
    left = jnp.maximum(i_f - first_needed - (ATTN_GROUP - 1.0), 0.0)
    ngroups = jnp.ceil(left * (1.0 / ATTN_GROUP)).astype(jnp.int32)[0, 0]
    npair = (ngroups + 1) // 2

    def body(u, carry):
        scores(2 * u + 2, sa_ref)
        softmax_group(2 * u + 1, sb_ref)
        scores(jnp.minimum(2 * u + 3, 2 * npair - 1), sb_ref)
        softmax_group(2 * u + 2, sa_ref)
        return carry

    lax.fori_loop(0, npair, body, 0)

    a0 = acc_ref[0:blk, :]
    a1 = acc_ref[blk:2 * blk, :]
    l0 = a0[:, HEAD_DIM:HEAD_DIM + 1]
    l1 = a1[:, 0:1]
    o_ref[0] = jnp.where(head0, a0 / l0, a1 / l1).astype(o_ref.dtype)


def _attention(q, k, v):
    b, s, d = q.shape
    blk = MOBA_BLOCK
    nb = s // blk
    assert s % blk == 0 and nb <= HEAD_DIM, "block-select lanes hold at most 64 blocks"
    slopes = 2.0 ** (-8.0 * jnp.arange(1, N_HEADS + 1, dtype=F32) / N_HEADS)
    islopes = 1.0 / (slopes * (LOG2E * blk))
    pairs = d // LANES
    sp = s + PAD_BLOCKS * blk
    full = pl.BlockSpec((1, s, LANES), lambda bi, pi, ii: (bi, 0, pi))
    tile = pl.BlockSpec((1, blk, LANES), lambda bi, pi, ii: (bi, ii, pi))
    smem = pl.BlockSpec(memory_space=pltpu.SMEM)
    return pl.pallas_call(
        functools.partial(_attn_kernel, nb=nb),
        grid=(b, pairs, nb),
        in_specs=[smem, smem, tile, full, full],
        out_specs=tile,
        out_shape=jax.ShapeDtypeStruct((b, s, d), BF16),
        scratch_shapes=[pltpu.VMEM((LANES, LANES), F32),
                        pltpu.VMEM((8, LANES), F32),
                        pltpu.VMEM((sp, 2 * LANES), BF16),
                        pltpu.VMEM((sp, LANES), BF16),
                        pltpu.VMEM((sp, LANES), BF16),
                        pltpu.VMEM((2 * blk, 2 * LANES), BF16),
                        pltpu.VMEM((2 * blk, ATTN_GROUP * blk), F32),
                        pltpu.VMEM((2 * blk, ATTN_GROUP * blk), F32),
                        pltpu.VMEM((2 * blk, 1), F32),
                        pltpu.VMEM((2 * blk, LANES), F32)],
        compiler_params=_cparams(3),
        name="moba_attn",
    )(slopes, islopes, q, k, v)


def _oproj_kernel(o_ref, x_ref, mod_ref, w_ref, out_ref):
    y = jnp.dot(o_ref[0], w_ref[...], preferred_element_type=F32)
    out_ref[0] = x_ref[0] + mod_ref[0][2:3] * y


def _oproj(o, x, mod, w_o):
    b, s, d = x.shape
    ts = 512
    blk = pl.BlockSpec((1, ts, d), lambda bi, si: (bi, si, 0))
    return pl.pallas_call(
        _oproj_kernel,
        grid=(b, s // ts),
        in_specs=[blk, blk,
                  pl.BlockSpec((1, 6, d), lambda bi, si: (bi, 0, 0)),
                  pl.BlockSpec((d, d), lambda bi, si: (0, 0))],
        out_specs=blk,
        out_shape=jax.ShapeDtypeStruct((b, s, d), F32),
        compiler_params=_cparams(2),
        name="attn_oproj",
    )(o, x, mod, w_o.astype(BF16))


def kernel(x, c, ada_w, ada_b, norm_mix_g, norm_ffn_g, pool_w, pool_scale, w_qkv, w_o,
           router_group_w, router_expert_w, exp_w_gate, exp_w_up, exp_w_down, final_norm_g):
    b, s, d = x.shape
    depth = ada_w.shape[0]
    mod = _adaln(c, ada_w, ada_b)
    for i in range(depth):
        if i % 2 == 0:
            x = _pool_layer(x, mod[i], norm_mix_g[i], pool_w[i // 2], pool_scale[i // 2])
        else:
            q, k, v = _qkv(x, mod[i], norm_mix_g[i], w_qkv[i // 2])
            o = _attention(q, k, v)
            x = _oproj(o, x, mod[i], w_o[i // 2])
        x = _moe_layer(x.reshape(b * s, d), mod[i], norm_ffn_g[i], router_group_w[i],
                       router_expert_w[i], exp_w_gate[i], exp_w_up[i], exp_w_down[i],
                       final_norm_g, s, final_norm=(i == depth - 1)).reshape(b, s, d)
    return x
```

```python
import functools

import jax
import jax.numpy as jnp
from jax import lax
from jax.experimental import pallas as pl
from jax.experimental.pallas import tpu as pltpu

F32 = jnp.float32
BF16 = jnp.bfloat16
HIGHEST = lax.Precision.HIGHEST

RMS_EPS = 1e-6
POOL_WINDOWS = (2, 4, 8, 16)
POOL_HALO = 16
N_HEADS = 16
HEAD_DIM = 64
MOBA_BLOCK = 256
MOBA_TOPK = 3
ATTN_GROUP = 4
N_EXPERT_GROUPS = 4
EXPERTS_PER_GROUP = 8
N_EXPERTS = N_EXPERT_GROUPS * EXPERTS_PER_GROUP
LANES = 128
NEG_BIG = -1e30
LOG2E = 1.4426950408889634
VMEM_LIMIT = 56 * 1024 * 1024


def _cparams(n_axes):
    return pltpu.CompilerParams(dimension_semantics=("arbitrary",) * n_axes,
                                vmem_limit_bytes=VMEM_LIMIT)


def _rms_mod(x, g, sc, sh):
    r = lax.rsqrt(jnp.mean(x * x, axis=-1, keepdims=True) + RMS_EPS)
    return (x * r) * g * (1.0 + sc) + sh


def _adaln_kernel(c_ref, w_ref, b_ref, o_ref):
    c = c_ref[...]
    ca = c / (1.0 + jnp.exp(-c))
    o_ref[0] = jnp.dot(ca, w_ref[0], precision=HIGHEST, preferred_element_type=F32) + b_ref[0]


def _adaln(c, ada_w, ada_b):
    depth, d, n = ada_w.shape
    b = c.shape[0]
    rows = 8
    cp = jnp.zeros((rows, d), F32).at[:b].set(c)
    tn = 1536
    out = pl.pallas_call(
        _adaln_kernel,
        grid=(depth, n // tn),
        in_specs=[pl.BlockSpec((rows, d), lambda i, j: (0, 0)),
                  pl.BlockSpec((1, d, tn), lambda i, j: (i, 0, j)),
                  pl.BlockSpec((1, 1, tn), lambda i, j: (i, 0, j))],
        out_specs=pl.BlockSpec((1, rows, tn), lambda i, j: (i, 0, j)),
        out_shape=jax.ShapeDtypeStruct((depth, rows, n), F32),
        compiler_params=_cparams(2),
        name="adaln",
    )(cp, ada_w, ada_b.reshape(depth, 1, n))
    return out[:, :b].reshape(depth, b, 6, d)


def _pool_kernel(x_ref, mod_ref, g_ref, w_ref, ps_ref, o_ref, hb_ref, *, ts):
    s = pl.program_id(1)

    @pl.when(s == 0)
    def _():
        hb_ref[0:POOL_HALO, :] = jnp.zeros((POOL_HALO, hb_ref.shape[1]), F32)

    x = x_ref[0]
    mod = mod_ref[0]
    h = _rms_mod(x, g_ref[...], mod[1:2], mod[0:1])
    hb_ref[POOL_HALO:POOL_HALO + ts, :] = h
    pos = s * ts + lax.broadcasted_iota(jnp.int32, (ts, 1), 0)
    gd = x.shape[1] // len(POOL_WINDOWS)
    ys = []
    for gi, w in enumerate(POOL_WINDOWS):
        c0 = gi * gd
        hg = h[:, c0:c0 + gd]
        acc = hg
        for k in range(1, w):
            acc = acc + hb_ref[POOL_HALO - k:POOL_HALO - k + ts, c0:c0 + gd]
        cnt = jnp.minimum(pos + 1, w).astype(F32)
        p = (acc / cnt - hg).astype(BF16)
        ys.append(jnp.dot(p, w_ref[gi], preferred_element_type=F32))
    y = jnp.concatenate(ys, axis=-1) * ps_ref[...]
    o_ref[0] = x + mod[2:3] * y
    hb_ref[0:POOL_HALO, :] = hb_ref[ts:ts + POOL_HALO, :]


def _pool_layer(x, mod, norm_g, pool_w, pool_scale):
    b, s, d = x.shape
    ts = 512
    g = len(POOL_WINDOWS)
    return pl.pallas_call(
        functools.partial(_pool_kernel, ts=ts),
        grid=(b, s // ts),
        in_specs=[pl.BlockSpec((1, ts, d), lambda bi, si: (bi, si, 0)),
                  pl.BlockSpec((1, 6, d), lambda bi, si: (bi, 0, 0)),
                  pl.BlockSpec((1, d), lambda bi, si: (0, 0)),
                  pl.BlockSpec((g, d // g, d // g), lambda bi, si: (0, 0, 0)),
                  pl.BlockSpec((1, d), lambda bi, si: (0, 0))],
        out_specs=pl.BlockSpec((1, ts, d), lambda bi, si: (bi, si, 0)),
        out_shape=jax.ShapeDtypeStruct((b, s, d), F32),
        scratch_shapes=[pltpu.VMEM((POOL_HALO + ts, d), F32)],
        compiler_params=_cparams(2),
        name="pool_mixer",
    )(x, mod, norm_g.reshape(1, d), pool_w.astype(BF16), pool_scale.reshape(1, d))


def _route(logits):
    lane = lax.broadcasted_iota(jnp.int32, logits.shape, 1)
    big = jnp.int32(1 << 20)
    is_g = (lane >= N_EXPERTS) & (lane < N_EXPERTS + N_EXPERT_GROUPS)
    glog = jnp.where(is_g, logits, -jnp.inf)
    gmax = jnp.max(glog, axis=-1, keepdims=True)
    gsel = jnp.min(jnp.where(glog == gmax, lane, big), axis=-1, keepdims=True) - N_EXPERTS
    gsum = jnp.sum(jnp.exp(glog - gmax), axis=-1, keepdims=True)
    g_w = 1.0 / gsum
    lo = gsel * EXPERTS_PER_GROUP
    in_grp = (lane >= lo) & (lane < lo + EXPERTS_PER_GROUP)
    elog = jnp.where(in_grp, logits, -jnp.inf)
    m1 = jnp.max(elog, axis=-1, keepdims=True)
    i1 = jnp.min(jnp.where(elog == m1, lane, big), axis=-1, keepdims=True)
    elog2 = jnp.where(lane == i1, -jnp.inf, elog)
    m2 = jnp.max(elog2, axis=-1, keepdims=True)
    i2 = jnp.min(jnp.where(elog2 == m2, lane, big), axis=-1, keepdims=True)
    d = jnp.exp(m2 - m1)
    w1 = g_w / (1.0 + d)
    w2 = g_w * d / (1.0 + d)
    return i1, i2, w1, w2


MOE_TILE = 256
ROUTE_TILE = 512
MOVE_TILE = 256
_INFO_ROWS = 8


def _route_kernel(x_ref, mod_ref, g_ref, wr_ref, hp_ref, info_ref, wts_ref, cnt_ref, carry_ref):
    t = pl.program_id(0)

    @pl.when(t == 0)
    def _():
        carry_ref[...] = jnp.zeros_like(carry_ref)

    mod = mod_ref[0]
    h = _rms_mod(x_ref[...], g_ref[...], mod[4:5], mod[3:4])
    hp_ref[...] = h
    logits = jnp.dot(h, wr_ref[...], precision=HIGHEST, preferred_element_type=F32)
    i1, i2, w1, w2 = _route(logits)
    tm = logits.shape[0]
    lane = lax.broadcasted_iota(jnp.int32, logits.shape, 1)
    m1 = lane == i1
    m2 = lane == i2
    mask = jnp.where(m1 | m2, 1.0, 0.0)
    earlier = (lax.broadcasted_iota(jnp.int32, (tm, tm), 0)
               > lax.broadcasted_iota(jnp.int32, (tm, tm), 1))
    rank = jnp.dot(jnp.where(earlier, 1.0, 0.0).astype(BF16), mask.astype(BF16),
                   preferred_element_type=F32) + carry_ref[...]
    rank1 = jnp.sum(jnp.where(m1, rank, 0.0), axis=-1, keepdims=True)
    rank2 = jnp.sum(jnp.where(m2, rank, 0.0), axis=-1, keepdims=True)
    carry_ref[...] += jnp.sum(mask, axis=0, keepdims=True)
    cnt_ref[...] = carry_ref[...]
    cols = (i1.astype(F32), i2.astype(F32), rank1, rank2)
    z = jnp.zeros(logits.shape, F32)
    for c, col in enumerate(cols):
        z = jnp.where(lane == c, col, z)
    zt = z.T[0:_INFO_ROWS, :].astype(jnp.int32)
    for part in range(tm // MOVE_TILE):
        info_ref[part] = zt[:, part * MOVE_TILE:(part + 1) * MOVE_TILE]
    wts_ref[...] = jnp.where(lane == 0, w1, jnp.where(lane == 1, w2, 0.0))


def _row_positions(base_ref, info_ref, r):
    return (base_ref[info_ref[0, r]] + info_ref[2, r], base_ref[info_ref[1, r]] + info_ref[3, r])


def _load_info(info_hbm, info_ref, sem):
    cp = pltpu.make_async_copy(info_hbm.at[pl.program_id(0)], info_ref, sem)
    cp.start()
    cp.wait()


def _scatter_kernel(base_ref, info_hbm, hp_ref, hs_zero_ref, hs_ref, info_ref, isem, sem):
    del hs_zero_ref
    _load_info(info_hbm, info_ref, isem)

    def row_copy(r, pos):
        return pltpu.make_async_copy(hp_ref.at[pl.ds(r, 1), :], hs_ref.at[pl.ds(pos, 1), :], sem)

    def issue(r, carry):
        for pos in _row_positions(base_ref, info_ref, r):
            row_copy(r, pos).start()
        return carry

    def drain(r, carry):
        row_copy(0, 0).wait()
        row_copy(0, 0).wait()
        return carry

    lax.fori_loop(0, MOVE_TILE, issue, 0, unroll=8)
    lax.fori_loop(0, MOVE_TILE, drain, 0, unroll=8)


def _expert_kernel(te_ref, hs_ref, wgu_ref, wd_ref, ys_ref):
    del te_ref
    f = wd_ref.shape[1]
    h = hs_ref[...].astype(BF16)
    gu = jnp.dot(h, wgu_ref[0], preferred_element_type=F32)
    gt = gu[:, :f]
    act = (gt / (1.0 + jnp.exp(-gt))) * gu[:, f:]
    y = jnp.dot(act.astype(BF16), wd_ref[0], preferred_element_type=F32)
    ys_ref[...] = y


def _combine_kernel(base_ref, info_hbm, x_ref, wts_ref, mod_ref, fg_ref, ys_ref, o_ref,
                    info_ref, y1_ref, y2_ref, isem, sem, *, final_norm):
    _load_info(info_hbm, info_ref, isem)

    def row_copy(pos, dst_ref, r):
        return pltpu.make_async_copy(ys_ref.at[pl.ds(pos, 1), :], dst_ref.at[pl.ds(r, 1), :], sem)

    def issue(r, carry):
        p1, p2 = _row_positions(base_ref, info_ref, r)
        row_copy(p1, y1_ref, r).start()
        row_copy(p2, y2_ref, r).start()
        return carry

    def drain(r, carry):
        row_copy(0, y1_ref, 0).wait()
        row_copy(0, y2_ref, 0).wait()
        return carry

    lax.fori_loop(0, MOVE_TILE, issue, 0, unroll=8)
    lax.fori_loop(0, MOVE_TILE, drain, 0, unroll=8)
    wts = wts_ref[...]
    y = wts[:, 0:1] * y1_ref[...] + wts[:, 1:2] * y2_ref[...]
    xn = x_ref[...] + mod_ref[0][5:6] * y
    if final_norm:
        r = lax.rsqrt(jnp.mean(xn * xn, axis=-1, keepdims=True) + RMS_EPS)
        xn = (xn * r) * fg_ref[...]
    o_ref[...] = xn


def _moe_layer(x2, mod, norm_g, w_group, w_expert, w_gate, w_up, w_down, final_g, seq, final_norm):
    t, d = x2.shape
    e, _, f = w_gate.shape
    n_move = t // MOVE_TILE
    n_tiles = (2 * t) // MOE_TILE + e
    wr = jnp.zeros((d, LANES), F32).at[:, :e].set(w_expert).at[:, e:e + N_EXPERT_GROUPS].set(w_group)
    wgu = jnp.concatenate([w_gate, w_up], axis=-1).astype(BF16)
    wd = w_down.astype(BF16)
    anyspace = pl.BlockSpec(memory_space=pl.ANY)

    hp, info, wts, cnt = pl.pallas_call(
        _route_kernel,
        grid=(t // ROUTE_TILE,),
        in_specs=[pl.BlockSpec((ROUTE_TILE, d), lambda ti: (ti, 0)),
                  pl.BlockSpec((1, 6, d), lambda ti: (ti * ROUTE_TILE // seq, 0, 0)),
                  pl.BlockSpec((1, d), lambda ti: (0, 0)),
                  pl.BlockSpec((d, LANES), lambda ti: (0, 0))],
        out_specs=[pl.BlockSpec((ROUTE_TILE, d), lambda ti: (ti, 0)),
                   pl.BlockSpec((ROUTE_TILE // MOVE_TILE, _INFO_ROWS, MOVE_TILE), lambda ti: (ti, 0, 0)),
                   pl.BlockSpec((ROUTE_TILE, LANES), lambda ti: (ti, 0)),
                   pl.BlockSpec((1, LANES), lambda ti: (0, 0))],
        out_shape=[jax.ShapeDtypeStruct((t, d), F32),
                   jax.ShapeDtypeStruct((n_move, _INFO_ROWS, MOVE_TILE), jnp.int32),
                   jax.ShapeDtypeStruct((t, LANES), F32),
                   jax.ShapeDtypeStruct((1, LANES), F32)],
        scratch_shapes=[pltpu.VMEM((1, LANES), F32)],
        compiler_params=_cparams(1),
        name="moe_route",
    )(x2, mod, norm_g.reshape(1, d), wr)

    counts = cnt[0, :e].astype(jnp.int32)
    padded = (counts + MOE_TILE - 1) // MOE_TILE * MOE_TILE
    ends = jnp.cumsum(padded)
    base = ends - padded
    tile_expert = jnp.minimum(
        jnp.searchsorted(ends, jnp.arange(n_tiles, dtype=jnp.int32) * MOE_TILE, side="right"),
        e - 1).astype(jnp.int32)

    hs = pl.pallas_call(
        _scatter_kernel,
        grid_spec=pltpu.PrefetchScalarGridSpec(
            num_scalar_prefetch=1,
            grid=(n_move,),
            in_specs=[anyspace,
                      pl.BlockSpec((MOVE_TILE, d), lambda ti, b: (ti, 0)),
                      anyspace],
            out_specs=anyspace,
            scratch_shapes=[pltpu.SMEM((_INFO_ROWS, MOVE_TILE), jnp.int32),
                            pltpu.SemaphoreType.DMA, pltpu.SemaphoreType.DMA]),
        out_shape=jax.ShapeDtypeStruct((n_tiles * MOE_TILE, d), F32),
        input_output_aliases={3: 0},
        compiler_params=_cparams(1),
        name="moe_scatter",
    )(base, info, hp, jnp.zeros((n_tiles * MOE_TILE, d), F32))

    ys = pl.pallas_call(
        _expert_kernel,
        grid_spec=pltpu.PrefetchScalarGridSpec(
            num_scalar_prefetch=1,
            grid=(n_tiles,),
            in_specs=[pl.BlockSpec((MOE_TILE, d), lambda si, te: (si, 0)),
                      pl.BlockSpec((1, d, 2 * f), lambda si, te: (te[si], 0, 0)),
                      pl.BlockSpec((1, f, d), lambda si, te: (te[si], 0, 0))],
            out_specs=pl.BlockSpec((MOE_TILE, d), lambda si, te: (si, 0))),
        out_shape=jax.ShapeDtypeStruct((n_tiles * MOE_TILE, d), F32),
        compiler_params=_cparams(1),
        name="moe_experts",
    )(tile_expert, hs, wgu, wd)

    return pl.pallas_call(
        functools.partial(_combine_kernel, final_norm=final_norm),
        grid_spec=pltpu.PrefetchScalarGridSpec(
            num_scalar_prefetch=1,
            grid=(n_move,),
            in_specs=[anyspace,
                      pl.BlockSpec((MOVE_TILE, d), lambda ti, b: (ti, 0)),
                      pl.BlockSpec((MOVE_TILE, LANES), lambda ti, b: (ti, 0)),
                      pl.BlockSpec((1, 6, d), lambda ti, b: (ti * MOVE_TILE // seq, 0, 0)),
                      pl.BlockSpec((1, d), lambda ti, b: (0, 0)),
                      anyspace],
            out_specs=pl.BlockSpec((MOVE_TILE, d), lambda ti, b: (ti, 0)),
            scratch_shapes=[pltpu.SMEM((_INFO_ROWS, MOVE_TILE), jnp.int32),
                            pltpu.VMEM((MOVE_TILE, d), F32),
                            pltpu.VMEM((MOVE_TILE, d), F32),
                            pltpu.SemaphoreType.DMA, pltpu.SemaphoreType.DMA]),
        out_shape=jax.ShapeDtypeStruct((t, d), F32),
        compiler_params=_cparams(1),
        name="moe_combine_final" if final_norm else "moe_combine",
    )(base, info, x2, wts, mod, final_g.reshape(1, d), ys)


def _qkv_kernel(x_ref, mod_ref, g_ref, w_ref, q_ref, k_ref, v_ref):
    mod = mod_ref[0]
    h = _rms_mod(x_ref[0], g_ref[...], mod[1:2], mod[0:1]).astype(BF16)
    d = h.shape[1]
    q_ref[0] = jnp.dot(h, w_ref[:, 0:d], preferred_element_type=F32)
    k_ref[0] = jnp.dot(h, w_ref[:, d:2 * d], preferred_element_type=F32).astype(BF16)
    v_ref[0] = jnp.dot(h, w_ref[:, 2 * d:3 * d], preferred_element_type=F32).astype(BF16)


def _qkv(x, mod, norm_g, w_qkv):
    b, s, d = x.shape
    ts = 512
    blk = pl.BlockSpec((1, ts, d), lambda bi, si: (bi, si, 0))
    return pl.pallas_call(
        _qkv_kernel,
        grid=(b, s // ts),
        in_specs=[blk,
                  pl.BlockSpec((1, 6, d), lambda bi, si: (bi, 0, 0)),
                  pl.BlockSpec((1, d), lambda bi, si: (0, 0)),
                  pl.BlockSpec((d, 3 * d), lambda bi, si: (0, 0))],
        out_specs=[blk, blk, blk],
        out_shape=[jax.ShapeDtypeStruct((b, s, d), F32),
                   jax.ShapeDtypeStruct((b, s, d), BF16),
                   jax.ShapeDtypeStruct((b, s, d), BF16)],
        compiler_params=_cparams(2),
        name="qkv_proj",
    )(x, mod, norm_g.reshape(1, d), w_qkv.astype(BF16))


_SLOPE_LANE = 64
_BLKIDX_LANE = 67
_PAD_LANE = 70
PAD_BLOCKS = 2 * ATTN_GROUP
UNDERFLOW_LOG2 = -160.0
NORM_SLACK = 1.02


def _attn_kernel(slopes_ref, islopes_ref, q_ref, k_ref, v_ref, o_ref,
                 kmean_ref, knorm_ref, kext_ref, v0_ref, v1_ref, lhs_ref, sa_ref, sb_ref,
                 m_ref, acc_ref, *, nb):
    p = pl.program_id(1)
    i = pl.program_id(2)
    blk = MOBA_BLOCK
    pad = PAD_BLOCKS * blk
    lane = lax.broadcasted_iota(jnp.int32, (blk, LANES), 1)
    head0 = lane < HEAD_DIM

    @pl.when(i == 0)
    def _():
        kmean_ref[...] = jnp.zeros_like(kmean_ref)
        knorm_ref[...] = jnp.zeros_like(knorm_ref)
        kext_ref[0:pad, 0:LANES] = jnp.zeros((pad, LANES), BF16)
        lane_p = lax.broadcasted_iota(jnp.int32, (pad, LANES), 1)
        kext_ref[0:pad, LANES:2 * LANES] = jnp.where(lane_p == _PAD_LANE, 1.0, 0.0).astype(BF16)
        v0_ref[0:pad, :] = jnp.zeros((pad, LANES), BF16)
        v1_ref[0:pad, :] = jnp.zeros((pad, LANES), BF16)
        row = lax.broadcasted_iota(jnp.int32, (blk, LANES), 0).astype(F32)
        in_slope = (lane >= _SLOPE_LANE) & (lane < _SLOPE_LANE + 3)
        in_blk = (lane >= _BLKIDX_LANE) & (lane < _BLKIDX_LANE + 3)
        ones0 = jnp.where(lane == HEAD_DIM, 1.0, 0.0)
        ones1 = jnp.where(lane == 0, 1.0, 0.0)

        def build(j, carry):
            src = pl.ds(pl.multiple_of(j * blk, blk), blk)
            dst = pl.ds(pl.multiple_of(j * blk + pad, blk), blk)
            kj = k_ref[0, src, :]
            kf = kj.astype(F32)
            kmean_ref[pl.ds(j, 1), :] = jnp.mean(kf, axis=0, keepdims=True)
            ksq = kf * kf
            n0 = jnp.max(jnp.sum(jnp.where(head0, ksq, 0.0), axis=-1, keepdims=True),
                         axis=0, keepdims=True)
            n1 = jnp.max(jnp.sum(jnp.where(head0, 0.0, ksq), axis=-1, keepdims=True),
                         axis=0, keepdims=True)
            knorm_ref[0:1, :] = jnp.maximum(knorm_ref[0:1, :], n0)
            knorm_ref[1:2, :] = jnp.maximum(knorm_ref[1:2, :], n1)
            jf = jnp.full((blk, LANES), j, jnp.int32).astype(F32)
            extra = jnp.where(lane == j, 1.0, jnp.where(in_slope, row, jnp.where(in_blk, jf, 0.0)))
            kext_ref[dst, 0:LANES] = kj
            kext_ref[dst, LANES:2 * LANES] = extra.astype(BF16)
            vj = v_ref[0, src, :].astype(F32)
            v0_ref[dst, :] = jnp.where(head0, vj, ones0).astype(BF16)
            v1_ref[dst, :] = jnp.where(head0, ones1, vj).astype(BF16)
            return carry

        lax.fori_loop(0, nb, build, 0)

    nt = (((1,), (1,)), ((), ()))
    gw = ATTN_GROUP * blk
    q = q_ref[0] * (HEAD_DIM ** -0.5 * LOG2E)
    qs = jnp.concatenate([jnp.where(head0, q, 0.0), jnp.where(head0, 0.0, q)], axis=0)
    lane2 = lax.broadcasted_iota(jnp.int32, qs.shape, 1)
    top_half = lax.broadcasted_iota(jnp.int32, qs.shape, 0) < blk
    gate = lax.dot_general(kmean_ref[...], qs, nt, precision=HIGHEST, preferred_element_type=F32)
    blk_i = lax.broadcasted_iota(jnp.int32, gate.shape, 0)
    blk_f = blk_i.astype(F32)
    valid = blk_i < i
    gate = jnp.where(valid, gate, -jnp.inf)
    sel = blk_i == i
    for _ in range(MOBA_TOPK):
        mx = jnp.max(gate, axis=0, keepdims=True)
        idx = jnp.min(jnp.where(gate == mx, blk_f, 1e9), axis=0, keepdims=True)
        hit = blk_f == idx
        sel = sel | (hit & valid)
        gate = jnp.where(hit, -jnp.inf, gate)
    ext = jnp.where(sel, 0.0, NEG_BIG).T
    slope = jnp.where(top_half, slopes_ref[2 * p], slopes_ref[2 * p + 1]) * LOG2E
    s_hi = slope.astype(BF16).astype(F32)
    rem = slope - s_hi
    s_mid = rem.astype(BF16).astype(F32)
    s_lo = rem - s_mid
    for k, part in enumerate((s_hi, s_mid, s_lo)):
        ext = jnp.where(lane2 == _SLOPE_LANE + k, part, ext)
        ext = jnp.where(lane2 == _BLKIDX_LANE + k, part * float(blk), ext)
    ext = jnp.where(lane2 > _PAD_LANE, 0.0, ext)
    lhs_ref[:, 0:LANES] = qs.astype(BF16)
    lhs_ref[:, LANES:2 * LANES] = ext.astype(BF16)
    qnorm = jnp.sqrt(jnp.sum(qs * qs, axis=-1, keepdims=True))

    m_ref[...] = jnp.full(m_ref.shape, NEG_BIG, F32)
    acc_ref[...] = jnp.zeros_like(acc_ref)

    def group_rows(t):
        start = (i - ATTN_GROUP * (t + 1) + 1 + PAD_BLOCKS) * blk
        return pl.ds(pl.multiple_of(start, blk), gw)

    def scores(t, dst_ref):
        dst_ref[...] = lax.dot_general(lhs_ref[...], kext_ref[group_rows(t), :], nt,
                                       preferred_element_type=F32)

    def softmax_group(t, src_ref, own_block=False):
        rows = group_rows(t)
        for hh in range(2):
            r = slice(hh * blk, (hh + 1) * blk)
            s = src_ref[r, :]
            if own_block:
                qi = lax.broadcasted_iota(jnp.int32, s.shape, 0)
                ki = lax.broadcasted_iota(jnp.int32, s.shape, 1) - (gw - blk)
                s = jnp.where(ki <= qi, s, NEG_BIG)
            m_old = m_ref[r, :]
            m_new = jnp.maximum(m_old, jnp.max(s, axis=-1, keepdims=True))
            alpha = jnp.exp2(m_old - m_new)
            pm = jnp.exp2(s - m_new).astype(BF16)
            vref = v0_ref if hh == 0 else v1_ref
            pv = jnp.dot(pm, vref[rows, :], preferred_element_type=F32)
            acc_ref[r, :] = alpha * acc_ref[r, :] + pv
            m_ref[r, :] = m_new

    scores(0, sa_ref)
    scores(1, sb_ref)
    softmax_group(0, sa_ref, own_block=True)

    knorm = jnp.sqrt(knorm_ref[0:2, 0:1])
    excess = NORM_SLACK * qnorm * jnp.where(top_half[:, 0:1], knorm[0:1], knorm[1:2]) - m_ref[...]
    a0 = jnp.max(excess[0:blk], axis=0, keepdims=True)
    a1 = jnp.max(excess[blk:2 * blk], axis=0, keepdims=True)
    edge = (blk - 1.0) / blk
    x0 = (UNDERFLOW_LOG2 - a0) * islopes_ref[2 * p] - edge
    x1 = (UNDERFLOW_LOG2 - a1) * islopes_ref[2 * p + 1] - edge
    i_f = jnp.full((1, 1), i, jnp.int32).astype(F32)
    first_needed = jnp.clip(jnp.floor(jnp.minimum(x0, x1)), 0.0, i_f)
    left = jnp.maximum(i_f - first_needed - (ATTN_GROUP - 1.0), 0.0)
    ngroups = jnp.ceil(left * (1.0 / ATTN_GROUP)).astype(jnp.int32)[0, 0]
    npair = (ngroups + 1) // 2

    def body(u, carry):
        scores(2 * u + 2, sa_ref)
        softmax_group(2 * u + 1, sb_ref)
        scores(jnp.minimum(2 * u + 3, 2 * npair - 1), sb_ref)
        softmax_group(2 * u + 2, sa_ref)
        return carry

    lax.fori_loop(0, npair, body, 0)

    a0 = acc_ref[0:blk, :]
    a1 = acc_ref[blk:2 * blk, :]
    l0 = a0[:, HEAD_DIM:HEAD_DIM + 1]
    l1 = a1[:, 0:1]
    o_ref[0] = jnp.where(head0, a0 / l0, a1 / l1).astype(o_ref.dtype)


def _attention(q, k, v):
    b, s, d = q.shape
    blk = MOBA_BLOCK
    nb = s // blk
    assert s % blk == 0 and nb <= HEAD_DIM, "block-select lanes hold at most 64 blocks"
    slopes = 2.0 ** (-8.0 * jnp.arange(1, N_HEADS + 1, dtype=F32) / N_HEADS)
    islopes = 1.0 / (slopes * (LOG2E * blk))
    pairs = d // LANES
    sp = s + PAD_BLOCKS * blk
    full = pl.BlockSpec((1, s, LANES), lambda bi, pi, ii: (bi, 0, pi))
    tile = pl.BlockSpec((1, blk, LANES), lambda bi, pi, ii: (bi, ii, pi))
    smem = pl.BlockSpec(memory_space=pltpu.SMEM)
    return pl.pallas_call(
        functools.partial(_attn_kernel, nb=nb),
        grid=(b, pairs, nb),
        in_specs=[smem, smem, tile, full, full],
        out_specs=tile,
        out_shape=jax.ShapeDtypeStruct((b, s, d), BF16),
        scratch_shapes=[pltpu.VMEM((LANES, LANES), F32),
                        pltpu.VMEM((8, LANES), F32),
                        pltpu.VMEM((sp, 2 * LANES), BF16),
                        pltpu.VMEM((sp, LANES), BF16),
                        pltpu.VMEM((sp, LANES), BF16),
                        pltpu.VMEM((2 * blk, 2 * LANES), BF16),
                        pltpu.VMEM((2 * blk, ATTN_GROUP * blk), F32),
                        pltpu.VMEM((2 * blk, ATTN_GROUP * blk), F32),
                        pltpu.VMEM((2 * blk, 1), F32),
                        pltpu.VMEM((2 * blk, LANES), F32)],
        compiler_params=_cparams(3),
        name="moba_attn",
    )(slopes, islopes, q, k, v)


def _oproj_kernel(o_ref, x_ref, mod_ref, w_ref, out_ref):
    y = jnp.dot(o_ref[0], w_ref[...], preferred_element_type=F32)
    out_ref[0] = x_ref[0] + mod_ref[0][2:3] * y


def _oproj(o, x, mod, w_o):
    b, s, d = x.shape
    ts = 512
    blk = pl.BlockSpec((1, ts, d), lambda bi, si: (bi, si, 0))
    return pl.pallas_call(
        _oproj_kernel,
        grid=(b, s // ts),
        in_specs=[blk, blk,
                  pl.BlockSpec((1, 6, d), lambda bi, si: (bi, 0, 0)),
                  pl.BlockSpec((d, d), lambda bi, si: (0, 0))],
        out_specs=blk,
        out_shape=jax.ShapeDtypeStruct((b, s, d), F32),
        compiler_params=_cparams(2),
        name="attn_oproj",
    )(o, x, mod, w_o.astype(BF16))


def kernel(x, c, ada_w, ada_b, norm_mix_g, norm_ffn_g, pool_w, pool_scale, w_qkv, w_o,
           router_group_w, router_expert_w, exp_w_gate, exp_w_up, exp_w_down, final_norm_g):
    b, s, d = x.shape
    depth = ada_w.shape[0]
    mod = _adaln(c, ada_w, ada_b)
    for i in range(depth):
        if i % 2 == 0:
            x = _pool_layer(x, mod[i], norm_mix_g[i], pool_w[i // 2], pool_scale[i // 2])
        else:
            q, k, v = _qkv(x, mod[i], norm_mix_g[i], w_qkv[i // 2])
            o = _attention(q, k, v)
            x = _oproj(o, x, mod[i], w_o[i // 2])
        x = _moe_layer(x.reshape(b * s, d), mod[i], norm_ffn_g[i], router_group_w[i],
                       router_expert_w[i], exp_w_gate[i], exp_w_up[i], exp_w_down[i],
                       final_norm_g, s, final_norm=(i == depth - 1)).reshape(b, s, d)
    return x
```

```python
import functools

import jax
import jax.numpy as jnp
from jax import lax
from jax.experimental import pallas as pl
from jax.experimental.pallas import tpu as pltpu

F32 = jnp.float32
BF16 = jnp.bfloat16
HIGHEST = lax.Precision.HIGHEST

RMS_EPS = 1e-6
POOL_WINDOWS = (2, 4, 8, 16)
POOL_HALO = 16
N_HEADS = 16
HEAD_DIM = 64
MOBA_BLOCK = 256
MOBA_TOPK = 3
ATTN_GROUP = 4
N_EXPERT_GROUPS = 4
EXPERTS_PER_GROUP = 8
N_EXPERTS = N_EXPERT_GROUPS * EXPERTS_PER_GROUP
LANES = 128
NEG_BIG = -1e30
LOG2E = 1.4426950408889634
VMEM_LIMIT = 56 * 1024 * 1024


def _cparams(n_axes):
    return pltpu.CompilerParams(dimension_semantics=("arbitrary",) * n_axes,
                                vmem_limit_bytes=VMEM_LIMIT)


def _rms_mod(x, g, sc, sh):
    r = lax.rsqrt(jnp.mean(x * x, axis=-1, keepdims=True) + RMS_EPS)
    return (x * r) * g * (1.0 + sc) + sh


def _adaln_kernel(c_ref, w_ref, b_ref, o_ref):
    c = c_ref[...]
    ca = c / (1.0 + jnp.exp(-c))
    o_ref[0] = jnp.dot(ca, w_ref[0], precision=HIGHEST, preferred_element_type=F32) + b_ref[0]


def _adaln(c, ada_w, ada_b):
    depth, d, n = ada_w.shape
    b = c.shape[0]
    rows = 8
    cp = jnp.zeros((rows, d), F32).at[:b].set(c)
    tn = 1536
    out = pl.pallas_call(
        _adaln_kernel,
        grid=(depth, n // tn),
        in_specs=[pl.BlockSpec((rows, d), lambda i, j: (0, 0)),
                  pl.BlockSpec((1, d, tn), lambda i, j: (i, 0, j)),
                  pl.BlockSpec((1, 1, tn), lambda i, j: (i, 0, j))],
        out_specs=pl.BlockSpec((1, rows, tn), lambda i, j: (i, 0, j)),
        out_shape=jax.ShapeDtypeStruct((depth, rows, n), F32),
        compiler_params=_cparams(2),
        name="adaln",
    )(cp, ada_w, ada_b.reshape(depth, 1, n))
    return out[:, :b].reshape(depth, b, 6, d)


def _pool_kernel(x_ref, mod_ref, g_ref, w_ref, ps_ref, o_ref, hb_ref, *, ts):
    s = pl.program_id(1)

    @pl.when(s == 0)
    def _():
        hb_ref[0:POOL_HALO, :] = jnp.zeros((POOL_HALO, hb_ref.shape[1]), F32)

    x = x_ref[0]
    mod = mod_ref[0]
    h = _rms_mod(x, g_ref[...], mod[1:2], mod[0:1])
    hb_ref[POOL_HALO:POOL_HALO + ts, :] = h
    pos = s * ts + lax.broadcasted_iota(jnp.int32, (ts, 1), 0)
    gd = x.shape[1] // len(POOL_WINDOWS)
    ys = []
    for gi, w in enumerate(POOL_WINDOWS):
        c0 = gi * gd
        hg = h[:, c0:c0 + gd]
        acc = hg
        for k in range(1, w):
            acc = acc + hb_ref[POOL_HALO - k:POOL_HALO - k + ts, c0:c0 + gd]
        cnt = jnp.minimum(pos + 1, w).astype(F32)
        p = (acc / cnt - hg).astype(BF16)
        ys.append(jnp.dot(p, w_ref[gi], preferred_element_type=F32))
    y = jnp.concatenate(ys, axis=-1) * ps_ref[...]
    o_ref[0] = x + mod[2:3] * y
    hb_ref[0:POOL_HALO, :] = hb_ref[ts:ts + POOL_HALO, :]


def _pool_layer(x, mod, norm_g, pool_w, pool_scale):
    b, s, d = x.shape
    ts = 512
    g = len(POOL_WINDOWS)
    return pl.pallas_call(
        functools.partial(_pool_kernel, ts=ts),
        grid=(b, s // ts),
        in_specs=[pl.BlockSpec((1, ts, d), lambda bi, si: (bi, si, 0)),
                  pl.BlockSpec((1, 6, d), lambda bi, si: (bi, 0, 0)),
                  pl.BlockSpec((1, d), lambda bi, si: (0, 0)),
                  pl.BlockSpec((g, d // g, d // g), lambda bi, si: (0, 0, 0)),
                  pl.BlockSpec((1, d), lambda bi, si: (0, 0))],
        out_specs=pl.BlockSpec((1, ts, d), lambda bi, si: (bi, si, 0)),
        out_shape=jax.ShapeDtypeStruct((b, s, d), F32),
        scratch_shapes=[pltpu.VMEM((POOL_HALO + ts, d), F32)],
        compiler_params=_cparams(2),
        name="pool_mixer",
    )(x, mod, norm_g.reshape(1, d), pool_w.astype(BF16), pool_scale.reshape(1, d))


def _route(logits):
    lane = lax.broadcasted_iota(jnp.int32, logits.shape, 1).astype(F32)
    big = 1e9

    def first_argmax(v, vmax):
        return jnp.min(jnp.where(v == vmax, lane, big), axis=-1, keepdims=True)

    is_g = (lane >= N_EXPERTS) & (lane < N_EXPERTS + N_EXPERT_GROUPS)
    glog = jnp.where(is_g, logits, -jnp.inf)
    gmax = jnp.max(glog, axis=-1, keepdims=True)
    gsel = first_argmax(glog, gmax) - N_EXPERTS
    gsum = jnp.sum(jnp.exp(glog - gmax), axis=-1, keepdims=True)
    g_w = 1.0 / gsum
    lo = gsel * EXPERTS_PER_GROUP
    in_grp = (lane >= lo) & (lane < lo + EXPERTS_PER_GROUP)
    elog = jnp.where(in_grp, logits, -jnp.inf)
    m1 = jnp.max(elog, axis=-1, keepdims=True)
    i1 = first_argmax(elog, m1)
    elog2 = jnp.where(lane == i1, -jnp.inf, elog)
    m2 = jnp.max(elog2, axis=-1, keepdims=True)
    i2 = first_argmax(elog2, m2)
    d = jnp.exp(m2 - m1)
    w1 = g_w / (1.0 + d)
    w2 = g_w * d / (1.0 + d)
    return i1.astype(jnp.int32), i2.astype(jnp.int32), w1, w2


MOE_TILE = 256
ROUTE_TILE = 512
MOVE_TILE = 1024
_INFO_ROWS = 8


def _route_kernel(x_ref, mod_ref, g_ref, wr_ref, hp_ref, info_ref, wts_ref, cnt_ref, carry_ref):
    t = pl.program_id(0)

    @pl.when(t == 0)
    def _():
        carry_ref[...] = jnp.zeros_like(carry_ref)

    mod = mod_ref[0]
    h = _rms_mod(x_ref[...], g_ref[...], mod[4:5], mod[3:4])
    hp_ref[...] = h
    logits = jnp.dot(h, wr_ref[...], precision=HIGHEST, preferred_element_type=F32)
    i1, i2, w1, w2 = _route(logits)
    tm = logits.shape[0]
    lane = lax.broadcasted_iota(jnp.int32, logits.shape, 1)
    m1 = lane == i1
    m2 = lane == i2
    mask = jnp.where(m1 | m2, 1.0, 0.0)
    earlier = (lax.broadcasted_iota(jnp.int32, (tm, tm), 0)
               > lax.broadcasted_iota(jnp.int32, (tm, tm), 1))
    rank = jnp.dot(jnp.where(earlier, 1.0, 0.0).astype(BF16), mask.astype(BF16),
                   preferred_element_type=F32) + carry_ref[...]
    rank1 = jnp.sum(jnp.where(m1, rank, 0.0), axis=-1, keepdims=True)
    rank2 = jnp.sum(jnp.where(m2, rank, 0.0), axis=-1, keepdims=True)
    carry_ref[...] += jnp.sum(mask, axis=0, keepdims=True)
    cnt_ref[...] = carry_ref[...]
    cols = (i1.astype(F32), i2.astype(F32), rank1, rank2)
    z = jnp.zeros(logits.shape, F32)
    for c, col in enumerate(cols):
        z = jnp.where(lane == c, col, z)
    info_ref[0] = z.T[0:_INFO_ROWS, :].astype(jnp.int32)
    wts_ref[...] = jnp.where(lane == 0, w1, jnp.where(lane == 1, w2, 0.0))


def _row_positions(base_ref, info_ref, r):
    return (base_ref[info_ref[0, r]] + info_ref[2, r], base_ref[info_ref[1, r]] + info_ref[3, r])


def _load_info(info_hbm, info_ref, sem):
    cp = pltpu.make_async_copy(info_hbm.at[pl.program_id(0)], info_ref, sem)
    cp.start()
    cp.wait()


def _scatter_kernel(base_ref, ends_ref, info_hbm, hp_ref, hs_ref, info_ref, zero_ref, isem, sem,
                    *, n_experts, n_tiles):
    @pl.when(pl.program_id(0) == 0)
    def _():
        zero_ref[...] = jnp.zeros_like(zero_ref)

        def zero_tile(row0):
            return pltpu.make_async_copy(zero_ref, hs_ref.at[pl.ds(row0, MOE_TILE), :], sem)

        def issue_e(e, issued):
            end = ends_ref[e]
            start = jnp.where(e == 0, 0, ends_ref[jnp.maximum(e - 1, 0)])
            nonempty = end > start

            @pl.when(nonempty)
            def _():
                zero_tile(pl.multiple_of(end - MOE_TILE, MOE_TILE)).start()

            return issued + nonempty.astype(jnp.int32)

        def issue_tail(s, carry):
            zero_tile(pl.multiple_of(s * MOE_TILE, MOE_TILE)).start()
            return carry

        first_tail = ends_ref[n_experts - 1] // MOE_TILE
        issued = lax.fori_loop(0, n_experts, issue_e, 0)
        lax.fori_loop(first_tail, n_tiles, issue_tail, 0)

        def drain_zero(s, carry):
            zero_tile(0).wait()
            return carry

        lax.fori_loop(0, issued + n_tiles - first_tail, drain_zero, 0)

    _load_info(info_hbm, info_ref, isem)

    def row_copy(r, pos):
        return pltpu.make_async_copy(hp_ref.at[pl.ds(r, 1), :], hs_ref.at[pl.ds(pos, 1), :], sem)

    def issue(r, carry):
        for pos in _row_positions(base_ref, info_ref, r):
            row_copy(r, pos).start()
        return carry

    def drain(r, carry):
        row_copy(0, 0).wait()
        row_copy(0, 0).wait()
        return carry

    lax.fori_loop(0, MOVE_TILE, issue, 0, unroll=8)
    lax.fori_loop(0, MOVE_TILE, drain, 0, unroll=8)


def _expert_kernel(te_ref, hs_ref, wgu_ref, wd_ref, ys_ref):
    del te_ref
    f = wd_ref.shape[1]
    h = hs_ref[...].astype(BF16)
    gu = jnp.dot(h, wgu_ref[0], preferred_element_type=F32)
    gt = gu[:, :f]
    act = (gt / (1.0 + jnp.exp(-gt))) * gu[:, f:]
    y = jnp.dot(act.astype(BF16), wd_ref[0], preferred_element_type=F32)
    ys_ref[...] = y


def _combine_kernel(base_ref, info_hbm, x_ref, wts_ref, mod_ref, fg_ref, ys_ref, o_ref,
                    info_ref, y1_ref, y2_ref, isem, sem, *, final_norm):
    _load_info(info_hbm, info_ref, isem)

    def row_copy(pos, dst_ref, r):
        return pltpu.make_async_copy(ys_ref.at[pl.ds(pos, 1), :], dst_ref.at[pl.ds(r, 1), :], sem)

    def issue(r, carry):
        p1, p2 = _row_positions(base_ref, info_ref, r)
        row_copy(p1, y1_ref, r).start()
        row_copy(p2, y2_ref, r).start()
        return carry

    def drain(r, carry):
        row_copy(0, y1_ref, 0).wait()
        row_copy(0, y2_ref, 0).wait()
        return carry

    lax.fori_loop(0, MOVE_TILE, issue, 0, unroll=8)
    lax.fori_loop(0, MOVE_TILE, drain, 0, unroll=8)
    wts = wts_ref[...]
    y = wts[:, 0:1] * y1_ref[...] + wts[:, 1:2] * y2_ref[...]
    xn = x_ref[...] + mod_ref[0][5:6] * y
    if final_norm:
        r = lax.rsqrt(jnp.mean(xn * xn, axis=-1, keepdims=True) + RMS_EPS)
        xn = (xn * r) * fg_ref[...]
    o_ref[...] = xn


def _moe_layer(x2, mod, norm_g, w_group, w_expert, w_gate, w_up, w_down, final_g, seq, final_norm):
    t, d = x2.shape
    e, _, f = w_gate.shape
    n_move = t // MOVE_TILE
    route_per_move = MOVE_TILE // ROUTE_TILE
    n_tiles = (2 * t) // MOE_TILE + e
    wr = jnp.zeros((d, LANES), F32).at[:, :e].set(w_expert).at[:, e:e + N_EXPERT_GROUPS].set(w_group)
    wgu = jnp.concatenate([w_gate, w_up], axis=-1).astype(BF16)
    wd = w_down.astype(BF16)
    anyspace = pl.BlockSpec(memory_space=pl.ANY)

    hp, info, wts, cnt = pl.pallas_call(
        _route_kernel,
        grid=(t // ROUTE_TILE,),
        in_specs=[pl.BlockSpec((ROUTE_TILE, d), lambda ti: (ti, 0)),
                  pl.BlockSpec((1, 6, d), lambda ti: (ti * ROUTE_TILE // seq, 0, 0)),
                  pl.BlockSpec((1, d), lambda ti: (0, 0)),
                  pl.BlockSpec((d, LANES), lambda ti: (0, 0))],
        out_specs=[pl.BlockSpec((ROUTE_TILE, d), lambda ti: (ti, 0)),
                   pl.BlockSpec((1, _INFO_ROWS, ROUTE_TILE),
                                lambda ti: (ti // route_per_move, 0, ti % route_per_move)),
                   pl.BlockSpec((ROUTE_TILE, LANES), lambda ti: (ti, 0)),
                   pl.BlockSpec((1, LANES), lambda ti: (0, 0))],
        out_shape=[jax.ShapeDtypeStruct((t, d), F32),
                   jax.ShapeDtypeStruct((n_move, _INFO_ROWS, MOVE_TILE), jnp.int32),
                   jax.ShapeDtypeStruct((t, LANES), F32),
                   jax.ShapeDtypeStruct((1, LANES), F32)],
        scratch_shapes=[pltpu.VMEM((1, LANES), F32)],
        compiler_params=_cparams(1),
        name="moe_route",
    )(x2, mod, norm_g.reshape(1, d), wr)

    counts = cnt[0, :e].astype(jnp.int32)
    padded = (counts + MOE_TILE - 1) // MOE_TILE * MOE_TILE
    ends = jnp.cumsum(padded)
    base = ends - padded
    tile_start = jnp.arange(n_tiles, dtype=jnp.int32) * MOE_TILE
    tile_expert = jnp.minimum(jnp.sum((ends[None, :] <= tile_start[:, None]).astype(jnp.int32), axis=1),
                              e - 1)

    hs = pl.pallas_call(
        functools.partial(_scatter_kernel, n_experts=e, n_tiles=n_tiles),
        grid_spec=pltpu.PrefetchScalarGridSpec(
            num_scalar_prefetch=2,
            grid=(n_move,),
            in_specs=[anyspace,
                      pl.BlockSpec((MOVE_TILE, d), lambda ti, b, en: (ti, 0))],
            out_specs=anyspace,
            scratch_shapes=[pltpu.SMEM((_INFO_ROWS, MOVE_TILE), jnp.int32),
                            pltpu.VMEM((MOE_TILE, d), F32),
                            pltpu.SemaphoreType.DMA, pltpu.SemaphoreType.DMA]),
        out_shape=jax.ShapeDtypeStruct((n_tiles * MOE_TILE, d), F32),
        compiler_params=_cparams(1),
        name="moe_scatter",
    )(base, ends, info, hp)

    ys = pl.pallas_call(
        _expert_kernel,
        grid_spec=pltpu.PrefetchScalarGridSpec(
            num_scalar_prefetch=1,
            grid=(n_tiles,),
            in_specs=[pl.BlockSpec((MOE_TILE, d), lambda si, te: (si, 0)),
                      pl.BlockSpec((1, d, 2 * f), lambda si, te: (te[si], 0, 0)),
                      pl.BlockSpec((1, f, d), lambda si, te: (te[si], 0, 0))],
            out_specs=pl.BlockSpec((MOE_TILE, d), lambda si, te: (si, 0))),
        out_shape=jax.ShapeDtypeStruct((n_tiles * MOE_TILE, d), F32),
        compiler_params=_cparams(1),
        name="moe_experts",
    )(tile_expert, hs, wgu, wd)

    return pl.pallas_call(
        functools.partial(_combine_kernel, final_norm=final_norm),
        grid_spec=pltpu.PrefetchScalarGridSpec(
            num_scalar_prefetch=1,
            grid=(n_move,),
            in_specs=[anyspace,
                      pl.BlockSpec((MOVE_TILE, d), lambda ti, b: (ti, 0)),
                      pl.BlockSpec((MOVE_TILE, LANES), lambda ti, b: (ti, 0)),
                      pl.BlockSpec((1, 6, d), lambda ti, b: (ti * MOVE_TILE // seq, 0, 0)),
                      pl.BlockSpec((1, d), lambda ti, b: (0, 0)),
                      anyspace],
            out_specs=pl.BlockSpec((MOVE_TILE, d), lambda ti, b: (ti, 0)),
            scratch_shapes=[pltpu.SMEM((_INFO_ROWS, MOVE_TILE), jnp.int32),
                            pltpu.VMEM((MOVE_TILE, d), F32),
                            pltpu.VMEM((MOVE_TILE, d), F32),
                            pltpu.SemaphoreType.DMA, pltpu.SemaphoreType.DMA]),
        out_shape=jax.ShapeDtypeStruct((t, d), F32),
        compiler_params=_cparams(1),
        name="moe_combine_final" if final_norm else "moe_combine",
    )(base, info, x2, wts, mod, final_g.reshape(1, d), ys)


def _qkv_kernel(x_ref, mod_ref, g_ref, w_ref, q_ref, k_ref, v_ref):
    mod = mod_ref[0]
    h = _rms_mod(x_ref[0], g_ref[...], mod[1:2], mod[0:1]).astype(BF16)
    d = h.shape[1]
    q_ref[0] = jnp.dot(h, w_ref[:, 0:d], preferred_element_type=F32)
    k_ref[0] = jnp.dot(h, w_ref[:, d:2 * d], preferred_element_type=F32).astype(BF16)
    v_ref[0] = jnp.dot(h, w_ref[:, 2 * d:3 * d], preferred_element_type=F32).astype(BF16)


def _qkv(x, mod, norm_g, w_qkv):
    b, s, d = x.shape
    ts = 512
    blk = pl.BlockSpec((1, ts, d), lambda bi, si: (bi, si, 0))
    return pl.pallas_call(
        _qkv_kernel,
        grid=(b, s // ts),
        in_specs=[blk,
                  pl.BlockSpec((1, 6, d), lambda bi, si: (bi, 0, 0)),
                  pl.BlockSpec((1, d), lambda bi, si: (0, 0)),
                  pl.BlockSpec((d, 3 * d), lambda bi, si: (0, 0))],
        out_specs=[blk, blk, blk],
        out_shape=[jax.ShapeDtypeStruct((b, s, d), F32),
                   jax.ShapeDtypeStruct((b, s, d), BF16),
                   jax.ShapeDtypeStruct((b, s, d), BF16)],
        compiler_params=_cparams(2),
        name="qkv_proj",
    )(x, mod, norm_g.reshape(1, d), w_qkv.astype(BF16))


_SLOPE_LANE = 64
_BLKIDX_LANE = 67
_PAD_LANE = 70
PAD_BLOCKS = 2 * ATTN_GROUP
UNDERFLOW_LOG2 = -160.0
NORM_SLACK = 1.02


def _attn_kernel(slopes_ref, islopes_ref, q_ref, k_ref, v_ref, o_ref,
                 kmean_ref, knorm_ref, kext_ref, v0_ref, v1_ref, lhs_ref, sa_ref, sb_ref,
                 m_ref, acc_ref, *, nb):
    p = pl.program_id(1)
    i = pl.program_id(2)
    blk = MOBA_BLOCK
    pad = PAD_BLOCKS * blk
    lane = lax.broadcasted_iota(jnp.int32, (blk, LANES), 1)
    head0 = lane < HEAD_DIM

    @pl.when(i == 0)
    def _():
        kmean_ref[...] = jnp.zeros_like(kmean_ref)
        knorm_ref[...] = jnp.zeros_like(knorm_ref)
        kext_ref[0:pad, 0:LANES] = jnp.zeros((pad, LANES), BF16)
        lane_p = lax.broadcasted_iota(jnp.int32, (pad, LANES), 1)
        kext_ref[0:pad, LANES:2 * LANES] = jnp.where(lane_p == _PAD_LANE, 1.0, 0.0).astype(BF16)
        v0_ref[0:pad, :] = jnp.zeros((pad, LANES), BF16)
        v1_ref[0:pad, :] = jnp.zeros((pad, LANES), BF16)
        row = lax.broadcasted_iota(jnp.int32, (blk, LANES), 0).astype(F32)
        in_slope = (lane >= _SLOPE_LANE) & (lane < _SLOPE_LANE + 3)
        in_blk = (lane >= _BLKIDX_LANE) & (lane < _BLKIDX_LANE + 3)
        ones0 = jnp.where(lane == HEAD_DIM, 1.0, 0.0)
        ones1 = jnp.where(lane == 0, 1.0, 0.0)

        def build(j, carry):
            src = pl.ds(pl.multiple_of(j * blk, blk), blk)
            dst = pl.ds(pl.multiple_of(j * blk + pad, blk), blk)
            kj = k_ref[0, src, :]
            kf = kj.astype(F32)
            kmean_ref[pl.ds(j, 1), :] = jnp.mean(kf, axis=0, keepdims=True)
            ksq = kf * kf
            n0 = jnp.max(jnp.sum(jnp.where(head0, ksq, 0.0), axis=-1, keepdims=True),
                         axis=0, keepdims=True)
            n1 = jnp.max(jnp.sum(jnp.where(head0, 0.0, ksq), axis=-1, keepdims=True),
                         axis=0, keepdims=True)
            knorm_ref[0:1, :] = jnp.maximum(knorm_ref[0:1, :], n0)
            knorm_ref[1:2, :] = jnp.maximum(knorm_ref[1:2, :], n1)
            jf = jnp.full((blk, LANES), j, jnp.int32).astype(F32)
            extra = jnp.where(lane == j, 1.0, jnp.where(in_slope, row, jnp.where(in_blk, jf, 0.0)))
            kext_ref[dst, 0:LANES] = kj
            kext_ref[dst, LANES:2 * LANES] = extra.astype(BF16)
            vj = v_ref[0, src, :].astype(F32)
            v0_ref[dst, :] = jnp.where(head0, vj, ones0).astype(BF16)
            v1_ref[dst, :] = jnp.where(head0, ones1, vj).astype(BF16)
            return carry

        lax.fori_loop(0, nb, build, 0)

    nt = (((1,), (1,)), ((), ()))
    gw = ATTN_GROUP * blk
    q = q_ref[0] * (HEAD_DIM ** -0.5 * LOG2E)
    qs = jnp.concatenate([jnp.where(head0, q, 0.0), jnp.where(head0, 0.0, q)], axis=0)
    lane2 = lax.broadcasted_iota(jnp.int32, qs.shape, 1)
    top_half = lax.broadcasted_iota(jnp.int32, qs.shape, 0) < blk
    gate = lax.dot_general(kmean_ref[...], qs, nt, precision=HIGHEST, preferred_element_type=F32)
    blk_i = lax.broadcasted_iota(jnp.int32, gate.shape, 0)
    blk_f = blk_i.astype(F32)
    valid = blk_i < i
    gate = jnp.where(valid, gate, -jnp.inf)
    sel = blk_i == i
    for _ in range(MOBA_TOPK):
        mx = jnp.max(gate, axis=0, keepdims=True)
        idx = jnp.min(jnp.where(gate == mx, blk_f, 1e9), axis=0, keepdims=True)
        hit = blk_f == idx
        sel = sel | (hit & valid)
        gate = jnp.where(hit, -jnp.inf, gate)
    ext = jnp.where(sel, 0.0, NEG_BIG).T
    slope = jnp.where(top_half, slopes_ref[2 * p], slopes_ref[2 * p + 1]) * LOG2E
    s_hi = slope.astype(BF16).astype(F32)
    rem = slope - s_hi
    s_mid = rem.astype(BF16).astype(F32)
    s_lo = rem - s_mid
    for k, part in enumerate((s_hi, s_mid, s_lo)):
        ext = jnp.where(lane2 == _SLOPE_LANE + k, part, ext)
        ext = jnp.where(lane2 == _BLKIDX_LANE + k, part * float(blk), ext)
    ext = jnp.where(lane2 > _PAD_LANE, 0.0, ext)
    lhs_ref[:, 0:LANES] = qs.astype(BF16)
    lhs_ref[:, LANES:2 * LANES] = ext.astype(BF16)
    qnorm = jnp.sqrt(jnp.sum(qs * qs, axis=-1, keepdims=True))

    m_ref[...] = jnp.full(m_ref.shape, NEG_BIG, F32)
    acc_ref[...] = jnp.zeros_like(acc_ref)

    def group_rows(t):
        start = (i - ATTN_GROUP * (t + 1) + 1 + PAD_BLOCKS) * blk
        return pl.ds(pl.multiple_of(start, blk), gw)

    def scores(t, dst_ref):
        dst_ref[...] = lax.dot_general(lhs_ref[...], kext_ref[group_rows(t), :], nt,
                                       preferred_element_type=F32)

    def softmax_group(t, src_ref, own_block=False):
        rows = group_rows(t)
        for hh in range(2):
            r = slice(hh * blk, (hh + 1) * blk)
            s = src_ref[r, :]
            if own_block:
                qi = lax.broadcasted_iota(jnp.int32, s.shape, 0)
                ki = lax.broadcasted_iota(jnp.int32, s.shape, 1) - (gw - blk)
                s = jnp.where(ki <= qi, s, NEG_BIG)
            m_old = m_ref[r, :]
            m_new = jnp.maximum(m_old, jnp.max(s, axis=-1, keepdims=True))
            alpha = jnp.exp2(m_old - m_new)
            pm = jnp.exp2(s - m_new).astype(BF16)
            vref = v0_ref if hh == 0 else v1_ref
            pv = jnp.dot(pm, vref[rows, :], preferred_element_type=F32)
            acc_ref[r, :] = alpha * acc_ref[r, :] + pv
            m_ref[r, :] = m_new

    scores(0, sa_ref)
    scores(1, sb_ref)
    softmax_group(0, sa_ref, own_block=True)

    knorm = jnp.sqrt(knorm_ref[0:2, 0:1])
    excess = NORM_SLACK * qnorm * jnp.where(top_half[:, 0:1], knorm[0:1], knorm[1:2]) - m_ref[...]
    a0 = jnp.max(excess[0:blk], axis=0, keepdims=True)
    a1 = jnp.max(excess[blk:2 * blk], axis=0, keepdims=True)
    edge = (blk - 1.0) / blk
    x0 = (UNDERFLOW_LOG2 - a0) * islopes_ref[2 * p] - edge
    x1 = (UNDERFLOW_LOG2 - a1) * islopes_ref[2 * p + 1] - edge
    i_f = jnp.full((1, 1), i, jnp.int32).astype(F32)
    first_needed = jnp.clip(jnp.floor(jnp.minimum(x0, x1)), 0.0, i_f)
    left = jnp.maximum(i_f - first_needed - (ATTN_GROUP - 1.0), 0.0)
    ngroups = jnp.ceil(left * (1.0 / ATTN_GROUP)).astype(jnp.int32)[0, 0]
    npair = (ngroups + 1) // 2

    def body(u, carry):
        scores(2 * u + 2, sa_ref)
        softmax_group(2 * u + 1, sb_ref)
        scores(jnp.minimum(2 * u + 3, 2 * npair - 1), sb_ref)
        softmax_group(2 * u + 2, sa_ref)
        return carry

    lax.fori_loop(0, npair, body, 0)

    a0 = acc_ref[0:blk, :]
    a1 = acc_ref[blk:2 * blk, :]
    l0 = a0[:, HEAD_DIM:HEAD_DIM + 1]
    l1 = a1[:, 0:1]
    o_ref[0] = jnp.where(head0, a0 / l0, a1 / l1).astype(o_ref.dtype)


def _attention(q, k, v):
    b, s, d = q.shape
    blk = MOBA_BLOCK
    nb = s // blk
    assert s % blk == 0 and nb <= HEAD_DIM, "block-select lanes hold at most 64 blocks"
    slopes = 2.0 ** (-8.0 * jnp.arange(1, N_HEADS + 1, dtype=F32) / N_HEADS)
    islopes = 1.0 / (slopes * (LOG2E * blk))
    pairs = d // LANES
    sp = s + PAD_BLOCKS * blk
    full = pl.BlockSpec((1, s, LANES), lambda bi, pi, ii: (bi, 0, pi))
    tile = pl.BlockSpec((1, blk, LANES), lambda bi, pi, ii: (bi, ii, pi))
    smem = pl.BlockSpec(memory_space=pltpu.SMEM)
    return pl.pallas_call(
        functools.partial(_attn_kernel, nb=nb),
        grid=(b, pairs, nb),
        in_specs=[smem, smem, tile, full, full],
        out_specs=tile,
        out_shape=jax.ShapeDtypeStruct((b, s, d), BF16),
        scratch_shapes=[pltpu.VMEM((LANES, LANES), F32),
                        pltpu.VMEM((8, LANES), F32),
                        pltpu.VMEM((sp, 2 * LANES), BF16),
                        pltpu.VMEM((sp, LANES), BF16),
                        pltpu.VMEM((sp, LANES), BF16),
                        pltpu.VMEM((2 * blk, 2 * LANES), BF16),
                        pltpu.VMEM((2 * blk, ATTN_GROUP * blk), F32),
                        pltpu.VMEM((2 * blk, ATTN_GROUP * blk), F32),
                        pltpu.VMEM((2 * blk, 1), F32),
                        pltpu.VMEM((2 * blk, LANES), F32)],
        compiler_params=_cparams(3),
        name="moba_attn",
    )(slopes, islopes, q, k, v)


def _oproj_kernel(o_ref, x_ref, mod_ref, w_ref, out_ref):
    y = jnp.dot(o_ref[0], w_ref[...], preferred_element_type=F32)
    out_ref[0] = x_ref[0] + mod_ref[0][2:3] * y


def _oproj(o, x, mod, w_o):
    b, s, d = x.shape
    ts = 512
    blk = pl.BlockSpec((1, ts, d), lambda bi, si: (bi, si, 0))
    return pl.pallas_call(
        _oproj_kernel,
        grid=(b, s // ts),
        in_specs=[blk, blk,
                  pl.BlockSpec((1, 6, d), lambda bi, si: (bi, 0, 0)),
                  pl.BlockSpec((d, d), lambda bi, si: (0, 0))],
        out_specs=blk,
        out_shape=jax.ShapeDtypeStruct((b, s, d), F32),
        compiler_params=_cparams(2),
        name="attn_oproj",
    )(o, x, mod, w_o.astype(BF16))


def kernel(x, c, ada_w, ada_b, norm_mix_g, norm_ffn_g, pool_w, pool_scale, w_qkv, w_o,
           router_group_w, router_expert_w, exp_w_gate, exp_w_up, exp_w_down, final_norm_g):
    b, s, d = x.shape
    depth = ada_w.shape[0]
    mod = _adaln(c, ada_w, ada_b)
    for i in range(depth):
        if i % 2 == 0:
            x = _pool_layer(x, mod[i], norm_mix_g[i], pool_w[i // 2], pool_scale[i // 2])
        else:
            q, k, v = _qkv(x, mod[i], norm_mix_g[i], w_qkv[i // 2])
            o = _attention(q, k, v)
            x = _oproj(o, x, mod[i], w_o[i // 2])
        x = _moe_layer(x.reshape(b * s, d), mod[i], norm_ffn_g[i], router_group_w[i],
                       router_expert_w[i], exp_w_gate[i], exp_w_up[i], exp_w_down[i],
                       final_norm_g, s, final_norm=(i == depth - 1)).reshape(b, s, d)
    return x
```

```python
import functools

import jax
import jax.numpy as jnp
from jax import lax
from jax.experimental import pallas as pl
from jax.experimental.pallas import tpu as pltpu

F32 = jnp.float32
BF16 = jnp.bfloat16
HIGHEST = lax.Precision.HIGHEST

RMS_EPS = 1e-6
POOL_WINDOWS = (2, 4, 8, 16)
POOL_HALO = 16
N_HEADS = 16
HEAD_DIM = 64
MOBA_BLOCK = 256
MOBA_TOPK = 3
ATTN_GROUP = 4
N_EXPERT_GROUPS = 4
EXPERTS_PER_GROUP = 8
N_EXPERTS = N_EXPERT_GROUPS * EXPERTS_PER_GROUP
LANES = 128
NEG_BIG = -1e30
LOG2E = 1.4426950408889634
VMEM_LIMIT = 56 * 1024 * 1024


def _cparams(n_axes):
    return pltpu.CompilerParams(dimension_semantics=("arbitrary",) * n_axes,
                                vmem_limit_bytes=VMEM_LIMIT)


def _rms_mod(x, g, sc, sh):
    r = lax.rsqrt(jnp.mean(x * x, axis=-1, keepdims=True) + RMS_EPS)
    return (x * r) * g * (1.0 + sc) + sh


def _adaln_kernel(c_ref, w_ref, b_ref, o_ref):
    c = c_ref[...]
    ca = c / (1.0 + jnp.exp(-c))
    o_ref[0] = jnp.dot(ca, w_ref[0], precision=HIGHEST, preferred_element_type=F32) + b_ref[0]


def _adaln(c, ada_w, ada_b):
    depth, d, n = ada_w.shape
    b = c.shape[0]
    rows = 8
    cp = jnp.zeros((rows, d), F32).at[:b].set(c)
    tn = 1536
    out = pl.pallas_call(
        _adaln_kernel,
        grid=(depth, n // tn),
        in_specs=[pl.BlockSpec((rows, d), lambda i, j: (0, 0)),
                  pl.BlockSpec((1, d, tn), lambda i, j: (i, 0, j)),
                  pl.BlockSpec((1, 1, tn), lambda i, j: (i, 0, j))],
        out_specs=pl.BlockSpec((1, rows, tn), lambda i, j: (i, 0, j)),
        out_shape=jax.ShapeDtypeStruct((depth, rows, n), F32),
        compiler_params=_cparams(2),
        name="adaln",
    )(cp, ada_w, ada_b.reshape(depth, 1, n))
    return out[:, :b].reshape(depth, b, 6, d)


def _pool_kernel(x_ref, mod_ref, g_ref, w_ref, ps_ref, o_ref, hb_ref, *, ts):
    s = pl.program_id(1)

    @pl.when(s == 0)
    def _():
        hb_ref[0:POOL_HALO, :] = jnp.zeros((POOL_HALO, hb_ref.shape[1]), F32)

    x = x_ref[0]
    mod = mod_ref[0]
    h = _rms_mod(x, g_ref[...], mod[1:2], mod[0:1])
    hb_ref[POOL_HALO:POOL_HALO + ts, :] = h
    pos = s * ts + lax.broadcasted_iota(jnp.int32, (ts, 1), 0)
    gd = x.shape[1] // len(POOL_WINDOWS)
    ys = []
    for gi, w in enumerate(POOL_WINDOWS):
        c0 = gi * gd
        hg = h[:, c0:c0 + gd]
        acc = hg
        for k in range(1, w):
            acc = acc + hb_ref[POOL_HALO - k:POOL_HALO - k + ts, c0:c0 + gd]
        cnt = jnp.minimum(pos + 1, w).astype(F32)
        p = (acc / cnt - hg).astype(BF16)
        ys.append(jnp.dot(p, w_ref[gi], preferred_element_type=F32))
    y = jnp.concatenate(ys, axis=-1) * ps_ref[...]
    o_ref[0] = x + mod[2:3] * y
    hb_ref[0:POOL_HALO, :] = hb_ref[ts:ts + POOL_HALO, :]


def _pool_layer(x, mod, norm_g, pool_w, pool_scale):
    b, s, d = x.shape
    ts = 512
    g = len(POOL_WINDOWS)
    return pl.pallas_call(
        functools.partial(_pool_kernel, ts=ts),
        grid=(b, s // ts),
        in_specs=[pl.BlockSpec((1, ts, d), lambda bi, si: (bi, si, 0)),
                  pl.BlockSpec((1, 6, d), lambda bi, si: (bi, 0, 0)),
                  pl.BlockSpec((1, d), lambda bi, si: (0, 0)),
                  pl.BlockSpec((g, d // g, d // g), lambda bi, si: (0, 0, 0)),
                  pl.BlockSpec((1, d), lambda bi, si: (0, 0))],
        out_specs=pl.BlockSpec((1, ts, d), lambda bi, si: (bi, si, 0)),
        out_shape=jax.ShapeDtypeStruct((b, s, d), F32),
        scratch_shapes=[pltpu.VMEM((POOL_HALO + ts, d), F32)],
        compiler_params=_cparams(2),
        name="pool_mixer",
    )(x, mod, norm_g.reshape(1, d), pool_w.astype(BF16), pool_scale.reshape(1, d))


def _route(logits):
    lane = lax.broadcasted_iota(jnp.int32, logits.shape, 1).astype(F32)
    big = 1e9

    def first_argmax(v, vmax):
        return jnp.min(jnp.where(v == vmax, lane, big), axis=-1, keepdims=True)

    is_g = (lane >= N_EXPERTS) & (lane < N_EXPERTS + N_EXPERT_GROUPS)
    glog = jnp.where(is_g, logits, -jnp.inf)
    gmax = jnp.max(glog, axis=-1, keepdims=True)
    gsel = first_argmax(glog, gmax) - N_EXPERTS
    gsum = jnp.sum(jnp.exp(glog - gmax), axis=-1, keepdims=True)
    g_w = 1.0 / gsum
    lo = gsel * EXPERTS_PER_GROUP
    in_grp = (lane >= lo) & (lane < lo + EXPERTS_PER_GROUP)
    elog = jnp.where(in_grp, logits, -jnp.inf)
    m1 = jnp.max(elog, axis=-1, keepdims=True)
    i1 = first_argmax(elog, m1)
    elog2 = jnp.where(lane == i1, -jnp.inf, elog)
    m2 = jnp.max(elog2, axis=-1, keepdims=True)
    i2 = first_argmax(elog2, m2)
    d = jnp.exp(m2 - m1)
    w1 = g_w / (1.0 + d)
    w2 = g_w * d / (1.0 + d)
    return i1.astype(jnp.int32), i2.astype(jnp.int32), w1, w2


MOE_TILE = 256
ROUTE_TILE = 512
MOVE_TILE = 1024
_INFO_ROWS = 8
SUBLANES = 8


def _store_row_tiles(ref, v):
    groups = v.shape[1] // LANES
    for s in range(groups):
        ref[pl.ds(s, v.shape[0], stride=groups), :] = v[:, s * LANES:(s + 1) * LANES]


def _load_row_tiles(ref, groups):
    rows = ref.shape[0] // groups
    return jnp.concatenate([ref[pl.ds(s, rows, stride=groups), :] for s in range(groups)], axis=1)


def _row_tile(ref, r, groups):
    return ref.at[pl.ds(pl.multiple_of(r * groups, groups), groups), :]


def _route_kernel(x_ref, mod_ref, g_ref, wr_ref, hp_ref, info_ref, wts_ref, cnt_ref, carry_ref):
    t = pl.program_id(0)

    @pl.when(t == 0)
    def _():
        carry_ref[...] = jnp.zeros_like(carry_ref)

    mod = mod_ref[0]
    h = _rms_mod(x_ref[...], g_ref[...], mod[4:5], mod[3:4])
    _store_row_tiles(hp_ref, h)
    logits = jnp.dot(h, wr_ref[...], precision=HIGHEST, preferred_element_type=F32)
    i1, i2, w1, w2 = _route(logits)
    tm = logits.shape[0]
    lane = lax.broadcasted_iota(jnp.int32, logits.shape, 1)
    m1 = lane == i1
    m2 = lane == i2
    mask = jnp.where(m1 | m2, 1.0, 0.0)
    earlier = (lax.broadcasted_iota(jnp.int32, (tm, tm), 0)
               > lax.broadcasted_iota(jnp.int32, (tm, tm), 1))
    rank = jnp.dot(jnp.where(earlier, 1.0, 0.0).astype(BF16), mask.astype(BF16),
                   preferred_element_type=F32) + carry_ref[...]
    rank1 = jnp.sum(jnp.where(m1, rank, 0.0), axis=-1, keepdims=True)
    rank2 = jnp.sum(jnp.where(m2, rank, 0.0), axis=-1, keepdims=True)
    carry_ref[...] += jnp.sum(mask, axis=0, keepdims=True)
    cnt_ref[...] = carry_ref[...]
    cols = (i1.astype(F32), i2.astype(F32), rank1, rank2)
    z = jnp.zeros(logits.shape, F32)
    for c, col in enumerate(cols):
        z = jnp.where(lane == c, col, z)
    info_ref[0] = z.T[0:_INFO_ROWS, :].astype(jnp.int32)
    wts_ref[...] = jnp.where(lane == 0, w1, jnp.where(lane == 1, w2, 0.0))


def _row_positions(base_ref, info_ref, r):
    return (base_ref[info_ref[0, r]] + info_ref[2, r], base_ref[info_ref[1, r]] + info_ref[3, r])


def _load_info(info_hbm, info_ref, sem):
    cp = pltpu.make_async_copy(info_hbm.at[pl.program_id(0)], info_ref, sem)
    cp.start()
    cp.wait()


def _scatter_kernel(base_ref, ends_ref, info_hbm, hp_ref, hs_ref, info_ref, zero_ref, isem, sem,
                    *, n_experts, n_tiles):
    @pl.when(pl.program_id(0) == 0)
    def _():
        zero_ref[...] = jnp.zeros_like(zero_ref)

        def zero_tile(row0):
            dst = hs_ref.at[pl.ds(pl.multiple_of(row0 * SUBLANES, SUBLANES), MOE_TILE * SUBLANES), :]
            return pltpu.make_async_copy(zero_ref, dst, sem)

        def issue_e(e, issued):
            end = ends_ref[e]
            start = jnp.where(e == 0, 0, ends_ref[jnp.maximum(e - 1, 0)])
            nonempty = end > start

            @pl.when(nonempty)
            def _():
                zero_tile(pl.multiple_of(end - MOE_TILE, MOE_TILE)).start()

            return issued + nonempty.astype(jnp.int32)

        def issue_tail(s, carry):
            zero_tile(pl.multiple_of(s * MOE_TILE, MOE_TILE)).start()
            return carry

        first_tail = ends_ref[n_experts - 1] // MOE_TILE
        issued = lax.fori_loop(0, n_experts, issue_e, 0)
        lax.fori_loop(first_tail, n_tiles, issue_tail, 0)

        def drain_zero(s, carry):
            zero_tile(0).wait()
            return carry

        lax.fori_loop(0, issued + n_tiles - first_tail, drain_zero, 0)

    _load_info(info_hbm, info_ref, isem)

    def row_copy(r, pos):
        return pltpu.make_async_copy(_row_tile(hp_ref, r, SUBLANES), _row_tile(hs_ref, pos, SUBLANES),
                                     sem)

    def issue(r, carry):
        for pos in _row_positions(base_ref, info_ref, r):
            row_copy(r, pos).start()
        return carry

    def drain(r, carry):
        row_copy(0, 0).wait()
        row_copy(0, 0).wait()
        return carry

    lax.fori_loop(0, MOVE_TILE, issue, 0, unroll=8)
    lax.fori_loop(0, MOVE_TILE, drain, 0, unroll=8)


def _expert_kernel(te_ref, hs_ref, wgu_ref, wd_ref, ys_ref):
    del te_ref
    f = wd_ref.shape[1]
    h = _load_row_tiles(hs_ref, SUBLANES).astype(BF16)
    gu = jnp.dot(h, wgu_ref[0], preferred_element_type=F32)
    gt = gu[:, :f]
    act = (gt / (1.0 + jnp.exp(-gt))) * gu[:, f:]
    y = jnp.dot(act.astype(BF16), wd_ref[0], preferred_element_type=F32)
    _store_row_tiles(ys_ref, y)


def _combine_kernel(base_ref, info_hbm, x_ref, wts_ref, mod_ref, fg_ref, ys_ref, o_ref,
                    info_ref, y1_ref, y2_ref, isem, sem, *, final_norm):
    _load_info(info_hbm, info_ref, isem)

    def row_copy(pos, dst_ref, r):
        return pltpu.make_async_copy(_row_tile(ys_ref, pos, SUBLANES), _row_tile(dst_ref, r, SUBLANES),
                                     sem)

    def issue(r, carry):
        p1, p2 = _row_positions(base_ref, info_ref, r)
        row_copy(p1, y1_ref, r).start()
        row_copy(p2, y2_ref, r).start()
        return carry

    def drain(r, carry):
        row_copy(0, y1_ref, 0).wait()
        row_copy(0, y2_ref, 0).wait()
        return carry

    lax.fori_loop(0, MOVE_TILE, issue, 0, unroll=8)
    lax.fori_loop(0, MOVE_TILE, drain, 0, unroll=8)
    wts = wts_ref[...]
    y = (wts[:, 0:1] * _load_row_tiles(y1_ref, SUBLANES)
         + wts[:, 1:2] * _load_row_tiles(y2_ref, SUBLANES))
    xn = x_ref[...] + mod_ref[0][5:6] * y
    if final_norm:
        r = lax.rsqrt(jnp.mean(xn * xn, axis=-1, keepdims=True) + RMS_EPS)
        xn = (xn * r) * fg_ref[...]
    o_ref[...] = xn


def _moe_layer(x2, mod, norm_g, w_group, w_expert, w_gate, w_up, w_down, final_g, seq, final_norm):
    t, d = x2.shape
    e, _, f = w_gate.shape
    assert d == SUBLANES * LANES, "a token row must fill exactly one (8, 128) tile"
    n_move = t // MOVE_TILE
    route_per_move = MOVE_TILE // ROUTE_TILE
    n_tiles = (2 * t) // MOE_TILE + e
    wr = jnp.zeros((d, LANES), F32).at[:, :e].set(w_expert).at[:, e:e + N_EXPERT_GROUPS].set(w_group)
    wgu = jnp.concatenate([w_gate, w_up], axis=-1).astype(BF16)
    wd = w_down.astype(BF16)
    anyspace = pl.BlockSpec(memory_space=pl.ANY)

    hp, info, wts, cnt = pl.pallas_call(
        _route_kernel,
        grid=(t // ROUTE_TILE,),
        in_specs=[pl.BlockSpec((ROUTE_TILE, d), lambda ti: (ti, 0)),
                  pl.BlockSpec((1, 6, d), lambda ti: (ti * ROUTE_TILE // seq, 0, 0)),
                  pl.BlockSpec((1, d), lambda ti: (0, 0)),
                  pl.BlockSpec((d, LANES), lambda ti: (0, 0))],
        out_specs=[pl.BlockSpec((ROUTE_TILE * SUBLANES, LANES), lambda ti: (ti, 0)),
                   pl.BlockSpec((1, _INFO_ROWS, ROUTE_TILE),
                                lambda ti: (ti // route_per_move, 0, ti % route_per_move)),
                   pl.BlockSpec((ROUTE_TILE, LANES), lambda ti: (ti, 0)),
                   pl.BlockSpec((1, LANES), lambda ti: (0, 0))],
        out_shape=[jax.ShapeDtypeStruct((t * SUBLANES, LANES), F32),
                   jax.ShapeDtypeStruct((n_move, _INFO_ROWS, MOVE_TILE), jnp.int32),
                   jax.ShapeDtypeStruct((t, LANES), F32),
                   jax.ShapeDtypeStruct((1, LANES), F32)],
        scratch_shapes=[pltpu.VMEM((1, LANES), F32)],
        compiler_params=_cparams(1),
        name="moe_route",
    )(x2, mod, norm_g.reshape(1, d), wr)

    counts = cnt[0, :e].astype(jnp.int32)
    padded = (counts + MOE_TILE - 1) // MOE_TILE * MOE_TILE
    ends = jnp.cumsum(padded)
    base = ends - padded
    tile_start = jnp.arange(n_tiles, dtype=jnp.int32) * MOE_TILE
    tile_expert = jnp.minimum(jnp.sum((ends[None, :] <= tile_start[:, None]).astype(jnp.int32), axis=1),
                              e - 1)

    hs = pl.pallas_call(
        functools.partial(_scatter_kernel, n_experts=e, n_tiles=n_tiles),
        grid_spec=pltpu.PrefetchScalarGridSpec(
            num_scalar_prefetch=2,
            grid=(n_move,),
            in_specs=[anyspace,
                      pl.BlockSpec((MOVE_TILE * SUBLANES, LANES), lambda ti, b, en: (ti, 0))],
            out_specs=anyspace,
            scratch_shapes=[pltpu.SMEM((_INFO_ROWS, MOVE_TILE), jnp.int32),
                            pltpu.VMEM((MOE_TILE * SUBLANES, LANES), F32),
                            pltpu.SemaphoreType.DMA, pltpu.SemaphoreType.DMA]),
        out_shape=jax.ShapeDtypeStruct((n_tiles * MOE_TILE * SUBLANES, LANES), F32),
        compiler_params=_cparams(1),
        name="moe_scatter",
    )(base, ends, info, hp)

    ys = pl.pallas_call(
        _expert_kernel,
        grid_spec=pltpu.PrefetchScalarGridSpec(
            num_scalar_prefetch=1,
            grid=(n_tiles,),
            in_specs=[pl.BlockSpec((MOE_TILE * SUBLANES, LANES), lambda si, te: (si, 0)),
                      pl.BlockSpec((1, d, 2 * f), lambda si, te: (te[si], 0, 0)),
                      pl.BlockSpec((1, f, d), lambda si, te: (te[si], 0, 0))],
            out_specs=pl.BlockSpec((MOE_TILE * SUBLANES, LANES), lambda si, te: (si, 0))),
        out_shape=jax.ShapeDtypeStruct((n_tiles * MOE_TILE * SUBLANES, LANES), F32),
        compiler_params=_cparams(1),
        name="moe_experts",
    )(tile_expert, hs, wgu, wd)

    return pl.pallas_call(
        functools.partial(_combine_kernel, final_norm=final_norm),
        grid_spec=pltpu.PrefetchScalarGridSpec(
            num_scalar_prefetch=1,
            grid=(n_move,),
            in_specs=[anyspace,
                      pl.BlockSpec((MOVE_TILE, d), lambda ti, b: (ti, 0)),
                      pl.BlockSpec((MOVE_TILE, LANES), lambda ti, b: (ti, 0)),
                      pl.BlockSpec((1, 6, d), lambda ti, b: (ti * MOVE_TILE // seq, 0, 0)),
                      pl.BlockSpec((1, d), lambda ti, b: (0, 0)),
                      anyspace],
            out_specs=pl.BlockSpec((MOVE_TILE, d), lambda ti, b: (ti, 0)),
            scratch_shapes=[pltpu.SMEM((_INFO_ROWS, MOVE_TILE), jnp.int32),
                            pltpu.VMEM((MOVE_TILE * SUBLANES, LANES), F32),
                            pltpu.VMEM((MOVE_TILE * SUBLANES, LANES), F32),
                            pltpu.SemaphoreType.DMA, pltpu.SemaphoreType.DMA]),
        out_shape=jax.ShapeDtypeStruct((t, d), F32),
        compiler_params=_cparams(1),
        name="moe_combine_final" if final_norm else "moe_combine",
    )(base, info, x2, wts, mod, final_g.reshape(1, d), ys)


def _qkv_kernel(x_ref, mod_ref, g_ref, w_ref, q_ref, k_ref, v_ref):
    mod = mod_ref[0]
    h = _rms_mod(x_ref[0], g_ref[...], mod[1:2], mod[0:1]).astype(BF16)
    d = h.shape[1]
    q_ref[0] = jnp.dot(h, w_ref[:, 0:d], preferred_element_type=F32)
    k_ref[0] = jnp.dot(h, w_ref[:, d:2 * d], preferred_element_type=F32).astype(BF16)
    v_ref[0] = jnp.dot(h, w_ref[:, 2 * d:3 * d], preferred_element_type=F32).astype(BF16)


def _qkv(x, mod, norm_g, w_qkv):
    b, s, d = x.shape
    ts = 512
    blk = pl.BlockSpec((1, ts, d), lambda bi, si: (bi, si, 0))
    return pl.pallas_call(
        _qkv_kernel,
        grid=(b, s // ts),
        in_specs=[blk,
                  pl.BlockSpec((1, 6, d), lambda bi, si: (bi, 0, 0)),
                  pl.BlockSpec((1, d), lambda bi, si: (0, 0)),
                  pl.BlockSpec((d, 3 * d), lambda bi, si: (0, 0))],
        out_specs=[blk, blk, blk],
        out_shape=[jax.ShapeDtypeStruct((b, s, d), F32),
                   jax.ShapeDtypeStruct((b, s, d), BF16),
                   jax.ShapeDtypeStruct((b, s, d), BF16)],
        compiler_params=_cparams(2),
        name="qkv_proj",
    )(x, mod, norm_g.reshape(1, d), w_qkv.astype(BF16))


_SLOPE_LANE = 64
_BLKIDX_LANE = 67
_PAD_LANE = 70
PAD_BLOCKS = 2 * ATTN_GROUP
UNDERFLOW_LOG2 = -160.0
NORM_SLACK = 1.02


def _attn_kernel(slopes_ref, islopes_ref, q_ref, k_ref, v_ref, o_ref,
                 kmean_ref, knorm_ref, kext_ref, v0_ref, v1_ref, lhs_ref, sa_ref, sb_ref,
                 m_ref, acc_ref, *, nb):
    p = pl.program_id(1)
    i = pl.program_id(2)
    blk = MOBA_BLOCK
    pad = PAD_BLOCKS * blk
    lane = lax.broadcasted_iota(jnp.int32, (blk, LANES), 1)
    head0 = lane < HEAD_DIM

    @pl.when(i == 0)
    def _():
        kmean_ref[...] = jnp.zeros_like(kmean_ref)
        knorm_ref[...] = jnp.zeros_like(knorm_ref)
        kext_ref[0:pad, 0:LANES] = jnp.zeros((pad, LANES), BF16)
        lane_p = lax.broadcasted_iota(jnp.int32, (pad, LANES), 1)
        kext_ref[0:pad, LANES:2 * LANES] = jnp.where(lane_p == _PAD_LANE, 1.0, 0.0).astype(BF16)
        v0_ref[0:pad, :] = jnp.zeros((pad, LANES), BF16)
        v1_ref[0:pad, :] = jnp.zeros((pad, LANES), BF16)
        row = lax.broadcasted_iota(jnp.int32, (blk, LANES), 0).astype(F32)
        in_slope = (lane >= _SLOPE_LANE) & (lane < _SLOPE_LANE + 3)
        in_blk = (lane >= _BLKIDX_LANE) & (lane < _BLKIDX_LANE + 3)
        ones0 = jnp.where(lane == HEAD_DIM, 1.0, 0.0)
        ones1 = jnp.where(lane == 0, 1.0, 0.0)

        def build(j, carry):
            src = pl.ds(pl.multiple_of(j * blk, blk), blk)
            dst = pl.ds(pl.multiple_of(j * blk + pad, blk), blk)
            kj = k_ref[0, src, :]
            kf = kj.astype(F32)
            kmean_ref[pl.ds(j, 1), :] = jnp.mean(kf, axis=0, keepdims=True)
            ksq = kf * kf
            n0 = jnp.max(jnp.sum(jnp.where(head0, ksq, 0.0), axis=-1, keepdims=True),
                         axis=0, keepdims=True)
            n1 = jnp.max(jnp.sum(jnp.where(head0, 0.0, ksq), axis=-1, keepdims=True),
                         axis=0, keepdims=True)
            knorm_ref[0:1, :] = jnp.maximum(knorm_ref[0:1, :], n0)
            knorm_ref[1:2, :] = jnp.maximum(knorm_ref[1:2, :], n1)
            jf = jnp.full((blk, LANES), j, jnp.int32).astype(F32)
            extra = jnp.where(lane == j, 1.0, jnp.where(in_slope, row, jnp.where(in_blk, jf, 0.0)))
            kext_ref[dst, 0:LANES] = kj
            kext_ref[dst, LANES:2 * LANES] = extra.astype(BF16)
            vj = v_ref[0, src, :].astype(F32)
            v0_ref[dst, :] = jnp.where(head0, vj, ones0).astype(BF16)
            v1_ref[dst, :] = jnp.where(head0, ones1, vj).astype(BF16)
            return carry

        lax.fori_loop(0, nb, build, 0)

    nt = (((1,), (1,)), ((), ()))
    gw = ATTN_GROUP * blk
    q = q_ref[0] * (HEAD_DIM ** -0.5 * LOG2E)
    qs = jnp.concatenate([jnp.where(head0, q, 0.0), jnp.where(head0, 0.0, q)], axis=0)
    lane2 = lax.broadcasted_iota(jnp.int32, qs.shape, 1)
    top_half = lax.broadcasted_iota(jnp.int32, qs.shape, 0) < blk
    gate = lax.dot_general(kmean_ref[...], qs, nt, precision=HIGHEST, preferred_element_type=F32)
    blk_i = lax.broadcasted_iota(jnp.int32, gate.shape, 0)
    blk_f = blk_i.astype(F32)
    valid = blk_i < i
    gate = jnp.where(valid, gate, -jnp.inf)
    sel = blk_i == i
    for _ in range(MOBA_TOPK):
        mx = jnp.max(gate, axis=0, keepdims=True)
        idx = jnp.min(jnp.where(gate == mx, blk_f, 1e9), axis=0, keepdims=True)
        hit = blk_f == idx
        sel = sel | (hit & valid)
        gate = jnp.where(hit, -jnp.inf, gate)
    ext = jnp.where(sel, 0.0, NEG_BIG).T
    slope = jnp.where(top_half, slopes_ref[2 * p], slopes_ref[2 * p + 1]) * LOG2E
    s_hi = slope.astype(BF16).astype(F32)
    rem = slope - s_hi
    s_mid = rem.astype(BF16).astype(F32)
    s_lo = rem - s_mid
    for k, part in enumerate((s_hi, s_mid, s_lo)):
        ext = jnp.where(lane2 == _SLOPE_LANE + k, part, ext)
        ext = jnp.where(lane2 == _BLKIDX_LANE + k, part * float(blk), ext)
    ext = jnp.where(lane2 > _PAD_LANE, 0.0, ext)
    lhs_ref[:, 0:LANES] = qs.astype(BF16)
    lhs_ref[:, LANES:2 * LANES] = ext.astype(BF16)
    qnorm = jnp.sqrt(jnp.sum(qs * qs, axis=-1, keepdims=True))

    m_ref[...] = jnp.full(m_ref.shape, NEG_BIG, F32)
    acc_ref[...] = jnp.zeros_like(acc_ref)

    def group_rows(t):
        start = (i - ATTN_GROUP * (t + 1) + 1 + PAD_BLOCKS) * blk
        return pl.ds(pl.multiple_of(start, blk), gw)

    def scores(t, dst_ref):
        dst_ref[...] = lax.dot_general(lhs_ref[...], kext_ref[group_rows(t), :], nt,
                                       preferred_element_type=F32)

    def softmax_group(t, src_ref, own_block=False):
        rows = group_rows(t)
        for hh in range(2):
            r = slice(hh * blk, (hh + 1) * blk)
            s = src_ref[r, :]
            if own_block:
                qi = lax.broadcasted_iota(jnp.int32, s.shape, 0)
                ki = lax.broadcasted_iota(jnp.int32, s.shape, 1) - (gw - blk)
                s = jnp.where(ki <= qi, s, NEG_BIG)
            m_old = m_ref[r, :]
            m_new = jnp.maximum(m_old, jnp.max(s, axis=-1, keepdims=True))
            alpha = jnp.exp2(m_old - m_new)
            pm = jnp.exp2(s - m_new).astype(BF16)
            vref = v0_ref if hh == 0 else v1_ref
            pv = jnp.dot(pm, vref[rows, :], preferred_element_type=F32)
            acc_ref[r, :] = alpha * acc_ref[r, :] + pv
            m_ref[r, :] = m_new

    scores(0, sa_ref)
    scores(1, sb_ref)
    softmax_group(0, sa_ref, own_block=True)

    knorm = jnp.sqrt(knorm_ref[0:2, 0:1])
    excess = NORM_SLACK * qnorm * jnp.where(top_half[:, 0:1], knorm[0:1], knorm[1:2]) - m_ref[...]
    a0 = jnp.max(excess[0:blk], axis=0, keepdims=True)
    a1 = jnp.max(excess[blk:2 * blk], axis=0, keepdims=True)
    edge = (blk - 1.0) / blk
    x0 = (UNDERFLOW_LOG2 - a0) * islopes_ref[2 * p] - edge
    x1 = (UNDERFLOW_LOG2 - a1) * islopes_ref[2 * p + 1] - edge
    i_f = jnp.full((1, 1), i, jnp.int32).astype(F32)
    first_needed = jnp.clip(jnp.floor(jnp.minimum(x0, x1)), 0.0, i_f)
    left = jnp.maximum(i_f - first_needed - (ATTN_GROUP - 1.0), 0.0)
    ngroups = jnp.ceil(left * (1.0 / ATTN_GROUP)).astype(jnp.int32)[0, 0]
    npair = ngroups // 2

    def body(u, carry):
        scores(2 * u + 2, sa_ref)
        softmax_group(2 * u + 1, sb_ref)
        scores(jnp.minimum(2 * u + 3, ngroups), sb_ref)
        softmax_group(2 * u + 2, sa_ref)
        return carry

    lax.fori_loop(0, npair, body, 0)

    @pl.when((ngroups & 1) == 1)
    def _():
        softmax_group(ngroups, sb_ref)

    a0 = acc_ref[0:blk, :]
    a1 = acc_ref[blk:2 * blk, :]
    l0 = a0[:, HEAD_DIM:HEAD_DIM + 1]
    l1 = a1[:, 0:1]
    o_ref[0] = jnp.where(head0, a0 / l0, a1 / l1).astype(o_ref.dtype)


def _attention(q, k, v):
    b, s, d = q.shape
    blk = MOBA_BLOCK
    nb = s // blk
    assert s % blk == 0 and nb <= HEAD_DIM, "block-select lanes hold at most 64 blocks"
    slopes = 2.0 ** (-8.0 * jnp.arange(1, N_HEADS + 1, dtype=F32) / N_HEADS)
    islopes = 1.0 / (slopes * (LOG2E * blk))
    pairs = d // LANES
    sp = s + PAD_BLOCKS * blk
    full = pl.BlockSpec((1, s, LANES), lambda bi, pi, ii: (bi, 0, pi))
    tile = pl.BlockSpec((1, blk, LANES), lambda bi, pi, ii: (bi, ii, pi))
    smem = pl.BlockSpec(memory_space=pltpu.SMEM)
    return pl.pallas_call(
        functools.partial(_attn_kernel, nb=nb),
        grid=(b, pairs, nb),
        in_specs=[smem, smem, tile, full, full],
        out_specs=tile,
        out_shape=jax.ShapeDtypeStruct((b, s, d), BF16),
        scratch_shapes=[pltpu.VMEM((LANES, LANES), F32),
                        pltpu.VMEM((8, LANES), F32),
                        pltpu.VMEM((sp, 2 * LANES), BF16),
                        pltpu.VMEM((sp, LANES), BF16),
                        pltpu.VMEM((sp, LANES), BF16),
                        pltpu.VMEM((2 * blk, 2 * LANES), BF16),
                        pltpu.VMEM((2 * blk, ATTN_GROUP * blk), F32),
                        pltpu.VMEM((2 * blk, ATTN_GROUP * blk), F32),
                        pltpu.VMEM((2 * blk, 1), F32),
                        pltpu.VMEM((2 * blk, LANES), F32)],
        compiler_params=_cparams(3),
        name="moba_attn",
    )(slopes, islopes, q, k, v)


def _oproj_kernel(o_ref, x_ref, mod_ref, w_ref, out_ref):
    y = jnp.dot(o_ref[0], w_ref[...], preferred_element_type=F32)
    out_ref[0] = x_ref[0] + mod_ref[0][2:3] * y


def _oproj(o, x, mod, w_o):
    b, s, d = x.shape
    ts = 512
    blk = pl.BlockSpec((1, ts, d), lambda bi, si: (bi, si, 0))
    return pl.pallas_call(
        _oproj_kernel,
        grid=(b, s // ts),
        in_specs=[blk, blk,
                  pl.BlockSpec((1, 6, d), lambda bi, si: (bi, 0, 0)),
                  pl.BlockSpec((d, d), lambda bi, si: (0, 0))],
        out_specs=blk,
        out_shape=jax.ShapeDtypeStruct((b, s, d), F32),
        compiler_params=_cparams(2),
        name="attn_oproj",
    )(o, x, mod, w_o.astype(BF16))


def kernel(x, c, ada_w, ada_b, norm_mix_g, norm_ffn_g, pool_w, pool_scale, w_qkv, w_o,
           router_group_w, router_expert_w, exp_w_gate, exp_w_up, exp_w_down, final_norm_g):
    b, s, d = x.shape
    depth = ada_w.shape[0]
    mod = _adaln(c, ada_w, ada_b)
    for i in range(depth):
        if i % 2 == 0:
            x = _pool_layer(x, mod[i], norm_mix_g[i], pool_w[i // 2], pool_scale[i // 2])
        else:
            q, k, v = _qkv(x, mod[i], norm_mix_g[i], w_qkv[i // 2])
            o = _attention(q, k, v)
            x = _oproj(o, x, mod[i], w_o[i // 2])
        x = _moe_layer(x.reshape(b * s, d), mod[i], norm_ffn_g[i], router_group_w[i],
                       router_expert_w[i], exp_w_gate[i], exp_w_up[i], exp_w_down[i],
                       final_norm_g, s, final_norm=(i == depth - 1)).reshape(b, s, d)
    return x
```

```python
import functools

import jax
import jax.numpy as jnp
from jax import lax
from jax.experimental import pallas as pl
from jax.experimental.pallas import tpu as pltpu

F32 = jnp.float32
BF16 = jnp.bfloat16
HIGHEST = lax.Precision.HIGHEST

RMS_EPS = 1e-6
POOL_WINDOWS = (2, 4, 8, 16)
POOL_HALO = 16
N_HEADS = 16
HEAD_DIM = 64
MOBA_BLOCK = 256
MOBA_TOPK = 3
ATTN_GROUP = 4
N_EXPERT_GROUPS = 4
EXPERTS_PER_GROUP = 8
N_EXPERTS = N_EXPERT_GROUPS * EXPERTS_PER_GROUP
LANES = 128
NEG_BIG = -1e30
LOG2E = 1.4426950408889634
VMEM_LIMIT = 56 * 1024 * 1024


def _cparams(n_axes):
    return pltpu.CompilerParams(dimension_semantics=("arbitrary",) * n_axes,
                                vmem_limit_bytes=VMEM_LIMIT)


def _rms_mod(x, g, sc, sh):
    r = lax.rsqrt(jnp.mean(x * x, axis=-1, keepdims=True) + RMS_EPS)
    return (x * r) * g * (1.0 + sc) + sh


def _adaln_kernel(c_ref, w_ref, b_ref, o_ref):
    c = c_ref[...]
    ca = c / (1.0 + jnp.exp(-c))
    o_ref[0] = jnp.dot(ca, w_ref[0], precision=HIGHEST, preferred_element_type=F32) + b_ref[0]


def _adaln(c, ada_w, ada_b):
    depth, d, n = ada_w.shape
    b = c.shape[0]
    rows = 8
    cp = jnp.zeros((rows, d), F32).at[:b].set(c)
    tn = 1536
    out = pl.pallas_call(
        _adaln_kernel,
        grid=(depth, n // tn),
        in_specs=[pl.BlockSpec((rows, d), lambda i, j: (0, 0)),
                  pl.BlockSpec((1, d, tn), lambda i, j: (i, 0, j)),
                  pl.BlockSpec((1, 1, tn), lambda i, j: (i, 0, j))],
        out_specs=pl.BlockSpec((1, rows, tn), lambda i, j: (i, 0, j)),
        out_shape=jax.ShapeDtypeStruct((depth, rows, n), F32),
        compiler_params=_cparams(2),
        name="adaln",
    )(cp, ada_w, ada_b.reshape(depth, 1, n))
    return out[:, :b].reshape(depth, b, 6, d)


def _pool_kernel(x_ref, mod_ref, g_ref, w_ref, ps_ref, o_ref, hb_ref, *, ts):
    s = pl.program_id(1)

    @pl.when(s == 0)
    def _():
        hb_ref[0:POOL_HALO, :] = jnp.zeros((POOL_HALO, hb_ref.shape[1]), F32)

    x = x_ref[0]
    mod = mod_ref[0]
    h = _rms_mod(x, g_ref[...], mod[1:2], mod[0:1])
    hb_ref[POOL_HALO:POOL_HALO + ts, :] = h
    pos = s * ts + lax.broadcasted_iota(jnp.int32, (ts, 1), 0)
    gd = x.shape[1] // len(POOL_WINDOWS)
    ys = []
    for gi, w in enumerate(POOL_WINDOWS):
        c0 = gi * gd
        hg = h[:, c0:c0 + gd]
        acc = hg
        for k in range(1, w):
            acc = acc + hb_ref[POOL_HALO - k:POOL_HALO - k + ts, c0:c0 + gd]
        cnt = jnp.minimum(pos + 1, w).astype(F32)
        p = (acc / cnt - hg).astype(BF16)
        ys.append(jnp.dot(p, w_ref[gi], preferred_element_type=F32))
    y = jnp.concatenate(ys, axis=-1) * ps_ref[...]
    o_ref[0] = x + mod[2:3] * y
    hb_ref[0:POOL_HALO, :] = hb_ref[ts:ts + POOL_HALO, :]


def _pool_layer(x, mod, norm_g, pool_w, pool_scale):
    b, s, d = x.shape
    ts = 512
    g = len(POOL_WINDOWS)
    return pl.pallas_call(
        functools.partial(_pool_kernel, ts=ts),
        grid=(b, s // ts),
        in_specs=[pl.BlockSpec((1, ts, d), lambda bi, si: (bi, si, 0)),
                  pl.BlockSpec((1, 6, d), lambda bi, si: (bi, 0, 0)),
                  pl.BlockSpec((1, d), lambda bi, si: (0, 0)),
                  pl.BlockSpec((g, d // g, d // g), lambda bi, si: (0, 0, 0)),
                  pl.BlockSpec((1, d), lambda bi, si: (0, 0))],
        out_specs=pl.BlockSpec((1, ts, d), lambda bi, si: (bi, si, 0)),
        out_shape=jax.ShapeDtypeStruct((b, s, d), F32),
        scratch_shapes=[pltpu.VMEM((POOL_HALO + ts, d), F32)],
        compiler_params=_cparams(2),
        name="pool_mixer",
    )(x, mod, norm_g.reshape(1, d), pool_w.astype(BF16), pool_scale.reshape(1, d))


def _route(logits):
    lane = lax.broadcasted_iota(jnp.int32, logits.shape, 1).astype(F32)
    big = 1e9

    def first_argmax(v, vmax):
        return jnp.min(jnp.where(v == vmax, lane, big), axis=-1, keepdims=True)

    is_g = (lane >= N_EXPERTS) & (lane < N_EXPERTS + N_EXPERT_GROUPS)
    glog = jnp.where(is_g, logits, -jnp.inf)
    gmax = jnp.max(glog, axis=-1, keepdims=True)
    gsel = first_argmax(glog, gmax) - N_EXPERTS
    gsum = jnp.sum(jnp.exp(glog - gmax), axis=-1, keepdims=True)
    g_w = 1.0 / gsum
    lo = gsel * EXPERTS_PER_GROUP
    in_grp = (lane >= lo) & (lane < lo + EXPERTS_PER_GROUP)
    elog = jnp.where(in_grp, logits, -jnp.inf)
    m1 = jnp.max(elog, axis=-1, keepdims=True)
    i1 = first_argmax(elog, m1)
    elog2 = jnp.where(lane == i1, -jnp.inf, elog)
    m2 = jnp.max(elog2, axis=-1, keepdims=True)
    i2 = first_argmax(elog2, m2)
    d = jnp.exp(m2 - m1)
    w1 = g_w / (1.0 + d)
    w2 = g_w * d / (1.0 + d)
    return i1.astype(jnp.int32), i2.astype(jnp.int32), w1, w2


MOE_TILE = 256
ROUTE_TILE = 512
MOVE_TILE = 1024
_INFO_ROWS = 8
SUBLANES = 8


def _store_row_tiles(ref, v):
    groups = v.shape[1] // LANES
    for s in range(groups):
        ref[pl.ds(s, v.shape[0], stride=groups), :] = v[:, s * LANES:(s + 1) * LANES]


def _load_row_tiles(ref, groups):
    rows = ref.shape[0] // groups
    return jnp.concatenate([ref[pl.ds(s, rows, stride=groups), :] for s in range(groups)], axis=1)


def _row_tile(ref, r, groups):
    return ref.at[pl.ds(pl.multiple_of(r * groups, groups), groups), :]


def _route_kernel(x_ref, mod_ref, g_ref, wr_ref, hp_ref, info_ref, wts_ref, cnt_ref, carry_ref):
    t = pl.program_id(0)

    @pl.when(t == 0)
    def _():
        carry_ref[...] = jnp.zeros_like(carry_ref)

    mod = mod_ref[0]
    h = _rms_mod(x_ref[...], g_ref[...], mod[4:5], mod[3:4])
    _store_row_tiles(hp_ref, h)
    logits = jnp.dot(h, wr_ref[...], precision=HIGHEST, preferred_element_type=F32)
    i1, i2, w1, w2 = _route(logits)
    tm = logits.shape[0]
    lane = lax.broadcasted_iota(jnp.int32, logits.shape, 1)
    m1 = lane == i1
    m2 = lane == i2
    mask = jnp.where(m1 | m2, 1.0, 0.0)
    earlier = (lax.broadcasted_iota(jnp.int32, (tm, tm), 0)
               > lax.broadcasted_iota(jnp.int32, (tm, tm), 1))
    rank = jnp.dot(jnp.where(earlier, 1.0, 0.0).astype(BF16), mask.astype(BF16),
                   preferred_element_type=F32) + carry_ref[...]
    rank1 = jnp.sum(jnp.where(m1, rank, 0.0), axis=-1, keepdims=True)
    rank2 = jnp.sum(jnp.where(m2, rank, 0.0), axis=-1, keepdims=True)
    carry_ref[...] += jnp.sum(mask, axis=0, keepdims=True)
    cnt_ref[...] = carry_ref[...]
    cols = (i1.astype(F32), i2.astype(F32), rank1, rank2)
    z = jnp.zeros(logits.shape, F32)
    for c, col in enumerate(cols):
        z = jnp.where(lane == c, col, z)
    info_ref[0] = z.T[0:_INFO_ROWS, :].astype(jnp.int32)
    wts_ref[...] = jnp.where(lane == 0, w1, jnp.where(lane == 1, w2, 0.0))


def _row_positions(base_ref, info_ref, r):
    return (base_ref[info_ref[0, r]] + info_ref[2, r], base_ref[info_ref[1, r]] + info_ref[3, r])


def _load_info(info_hbm, info_ref, sem):
    cp = pltpu.make_async_copy(info_hbm.at[pl.program_id(0)], info_ref, sem)
    cp.start()
    cp.wait()


def _scatter_kernel(base_ref, ends_ref, info_hbm, hp_ref, hs_ref, info_ref, zero_ref, isem, sem,
                    *, n_experts, n_tiles):
    @pl.when(pl.program_id(0) == 0)
    def _():
        zero_ref[...] = jnp.zeros_like(zero_ref)

        def zero_tile(row0):
            dst = hs_ref.at[pl.ds(pl.multiple_of(row0 * SUBLANES, SUBLANES), MOE_TILE * SUBLANES), :]
            return pltpu.make_async_copy(zero_ref, dst, sem)

        def issue_e(e, issued):
            end = ends_ref[e]
            start = jnp.where(e == 0, 0, ends_ref[jnp.maximum(e - 1, 0)])
            nonempty = end > start

            @pl.when(nonempty)
            def _():
                zero_tile(pl.multiple_of(end - MOE_TILE, MOE_TILE)).start()

            return issued + nonempty.astype(jnp.int32)

        def issue_tail(s, carry):
            zero_tile(pl.multiple_of(s * MOE_TILE, MOE_TILE)).start()
            return carry

        first_tail = ends_ref[n_experts - 1] // MOE_TILE
        issued = lax.fori_loop(0, n_experts, issue_e, 0)
        lax.fori_loop(first_tail, n_tiles, issue_tail, 0)

        def drain_zero(s, carry):
            zero_tile(0).wait()
            return carry

        lax.fori_loop(0, issued + n_tiles - first_tail, drain_zero, 0)

    _load_info(info_hbm, info_ref, isem)

    def row_copy(r, pos):
        return pltpu.make_async_copy(_row_tile(hp_ref, r, SUBLANES), _row_tile(hs_ref, pos, SUBLANES),
                                     sem)

    def issue(r, carry):
        for pos in _row_positions(base_ref, info_ref, r):
            row_copy(r, pos).start()
        return carry

    def drain(r, carry):
        row_copy(0, 0).wait()
        row_copy(0, 0).wait()
        return carry

    lax.fori_loop(0, MOVE_TILE, issue, 0, unroll=8)
    lax.fori_loop(0, MOVE_TILE, drain, 0, unroll=8)


def _expert_kernel(te_ref, hs_ref, wg_ref, wu_ref, wd_ref, ys_ref):
    del te_ref
    h = _load_row_tiles(hs_ref, SUBLANES).astype(BF16)
    gt = jnp.dot(h, wg_ref[0, 0].astype(BF16), preferred_element_type=F32)
    up = jnp.dot(h, wu_ref[0, 0].astype(BF16), preferred_element_type=F32)
    act = (gt / (1.0 + jnp.exp(-gt))) * up
    y = jnp.dot(act.astype(BF16), wd_ref[0, 0].astype(BF16), preferred_element_type=F32)
    _store_row_tiles(ys_ref, y)


def _combine_kernel(base_ref, info_hbm, x_ref, wts_ref, mod_ref, fg_ref, ys_ref, o_ref,
                    info_ref, y1_ref, y2_ref, isem, sem, *, final_norm):
    _load_info(info_hbm, info_ref, isem)

    def row_copy(pos, dst_ref, r):
        return pltpu.make_async_copy(_row_tile(ys_ref, pos, SUBLANES), _row_tile(dst_ref, r, SUBLANES),
                                     sem)

    def issue(r, carry):
        p1, p2 = _row_positions(base_ref, info_ref, r)
        row_copy(p1, y1_ref, r).start()
        row_copy(p2, y2_ref, r).start()
        return carry

    def drain(r, carry):
        row_copy(0, y1_ref, 0).wait()
        row_copy(0, y2_ref, 0).wait()
        return carry

    lax.fori_loop(0, MOVE_TILE, issue, 0, unroll=8)
    lax.fori_loop(0, MOVE_TILE, drain, 0, unroll=8)
    wts = wts_ref[...]
    y = (wts[:, 0:1] * _load_row_tiles(y1_ref, SUBLANES)
         + wts[:, 1:2] * _load_row_tiles(y2_ref, SUBLANES))
    xn = x_ref[...] + mod_ref[0][5:6] * y
    if final_norm:
        r = lax.rsqrt(jnp.mean(xn * xn, axis=-1, keepdims=True) + RMS_EPS)
        xn = (xn * r) * fg_ref[...]
    o_ref[...] = xn


def _moe_layer(x2, mod, norm_g, w_group, w_expert, w_gate, w_up, w_down, layer, final_g, seq,
               final_norm):
    t, d = x2.shape
    _, e, _, f = w_gate.shape
    assert d == SUBLANES * LANES, "a token row must fill exactly one (8, 128) tile"
    n_move = t // MOVE_TILE
    route_per_move = MOVE_TILE // ROUTE_TILE
    n_tiles = (2 * t) // MOE_TILE + e
    wr = jnp.zeros((d, LANES), F32).at[:, :e].set(w_expert).at[:, e:e + N_EXPERT_GROUPS].set(w_group)
    anyspace = pl.BlockSpec(memory_space=pl.ANY)

    hp, info, wts, cnt = pl.pallas_call(
        _route_kernel,
        grid=(t // ROUTE_TILE,),
        in_specs=[pl.BlockSpec((ROUTE_TILE, d), lambda ti: (ti, 0)),
                  pl.BlockSpec((1, 6, d), lambda ti: (ti * ROUTE_TILE // seq, 0, 0)),
                  pl.BlockSpec((1, d), lambda ti: (0, 0)),
                  pl.BlockSpec((d, LANES), lambda ti: (0, 0))],
        out_specs=[pl.BlockSpec((ROUTE_TILE * SUBLANES, LANES), lambda ti: (ti, 0)),
                   pl.BlockSpec((1, _INFO_ROWS, ROUTE_TILE),
                                lambda ti: (ti // route_per_move, 0, ti % route_per_move)),
                   pl.BlockSpec((ROUTE_TILE, LANES), lambda ti: (ti, 0)),
                   pl.BlockSpec((1, LANES), lambda ti: (0, 0))],
        out_shape=[jax.ShapeDtypeStruct((t * SUBLANES, LANES), F32),
                   jax.ShapeDtypeStruct((n_move, _INFO_ROWS, MOVE_TILE), jnp.int32),
                   jax.ShapeDtypeStruct((t, LANES), F32),
                   jax.ShapeDtypeStruct((1, LANES), F32)],
        scratch_shapes=[pltpu.VMEM((1, LANES), F32)],
        compiler_params=_cparams(1),
        name="moe_route",
    )(x2, mod, norm_g.reshape(1, d), wr)

    counts = cnt[0, :e].astype(jnp.int32)
    padded = (counts + MOE_TILE - 1) // MOE_TILE * MOE_TILE
    ends = jnp.cumsum(padded)
    base = ends - padded
    tile_start = jnp.arange(n_tiles, dtype=jnp.int32) * MOE_TILE
    tile_expert = jnp.minimum(jnp.sum((ends[None, :] <= tile_start[:, None]).astype(jnp.int32), axis=1),
                              e - 1)

    hs = pl.pallas_call(
        functools.partial(_scatter_kernel, n_experts=e, n_tiles=n_tiles),
        grid_spec=pltpu.PrefetchScalarGridSpec(
            num_scalar_prefetch=2,
            grid=(n_move,),
            in_specs=[anyspace,
                      pl.BlockSpec((MOVE_TILE * SUBLANES, LANES), lambda ti, b, en: (ti, 0))],
            out_specs=anyspace,
            scratch_shapes=[pltpu.SMEM((_INFO_ROWS, MOVE_TILE), jnp.int32),
                            pltpu.VMEM((MOE_TILE * SUBLANES, LANES), F32),
                            pltpu.SemaphoreType.DMA, pltpu.SemaphoreType.DMA]),
        out_shape=jax.ShapeDtypeStruct((n_tiles * MOE_TILE * SUBLANES, LANES), F32),
        compiler_params=_cparams(1),
        name="moe_scatter",
    )(base, ends, info, hp)

    ys = pl.pallas_call(
        _expert_kernel,
        grid_spec=pltpu.PrefetchScalarGridSpec(
            num_scalar_prefetch=1,
            grid=(n_tiles,),
            in_specs=[pl.BlockSpec((MOE_TILE * SUBLANES, LANES), lambda si, te: (si, 0)),
                      pl.BlockSpec((1, 1, d, f), lambda si, te: (layer, te[si], 0, 0)),
                      pl.BlockSpec((1, 1, d, f), lambda si, te: (layer, te[si], 0, 0)),
                      pl.BlockSpec((1, 1, f, d), lambda si, te: (layer, te[si], 0, 0))],
            out_specs=pl.BlockSpec((MOE_TILE * SUBLANES, LANES), lambda si, te: (si, 0))),
        out_shape=jax.ShapeDtypeStruct((n_tiles * MOE_TILE * SUBLANES, LANES), F32),
        compiler_params=_cparams(1),
        name="moe_experts",
    )(tile_expert, hs, w_gate, w_up, w_down)

    return pl.pallas_call(
        functools.partial(_combine_kernel, final_norm=final_norm),
        grid_spec=pltpu.PrefetchScalarGridSpec(
            num_scalar_prefetch=1,
            grid=(n_move,),
            in_specs=[anyspace,
                      pl.BlockSpec((MOVE_TILE, d), lambda ti, b: (ti, 0)),
                      pl.BlockSpec((MOVE_TILE, LANES), lambda ti, b: (ti, 0)),
                      pl.BlockSpec((1, 6, d), lambda ti, b: (ti * MOVE_TILE // seq, 0, 0)),
                      pl.BlockSpec((1, d), lambda ti, b: (0, 0)),
                      anyspace],
            out_specs=pl.BlockSpec((MOVE_TILE, d), lambda ti, b: (ti, 0)),
            scratch_shapes=[pltpu.SMEM((_INFO_ROWS, MOVE_TILE), jnp.int32),
                            pltpu.VMEM((MOVE_TILE * SUBLANES, LANES), F32),
                            pltpu.VMEM((MOVE_TILE * SUBLANES, LANES), F32),
                            pltpu.SemaphoreType.DMA, pltpu.SemaphoreType.DMA]),
        out_shape=jax.ShapeDtypeStruct((t, d), F32),
        compiler_params=_cparams(1),
        name="moe_combine_final" if final_norm else "moe_combine",
    )(base, info, x2, wts, mod, final_g.reshape(1, d), ys)


def _qkv_kernel(x_ref, mod_ref, g_ref, w_ref, q_ref, k_ref, v_ref):
    mod = mod_ref[0]
    h = _rms_mod(x_ref[0], g_ref[...], mod[1:2], mod[0:1]).astype(BF16)
    d = h.shape[1]
    q_ref[0] = jnp.dot(h, w_ref[:, 0:d], preferred_element_type=F32)
    k_ref[0] = jnp.dot(h, w_ref[:, d:2 * d], preferred_element_type=F32).astype(BF16)
    v_ref[0] = jnp.dot(h, w_ref[:, 2 * d:3 * d], preferred_element_type=F32).astype(BF16)


def _qkv(x, mod, norm_g, w_qkv):
    b, s, d = x.shape
    ts = 512
    blk = pl.BlockSpec((1, ts, d), lambda bi, si: (bi, si, 0))
    return pl.pallas_call(
        _qkv_kernel,
        grid=(b, s // ts),
        in_specs=[blk,
                  pl.BlockSpec((1, 6, d), lambda bi, si: (bi, 0, 0)),
                  pl.BlockSpec((1, d), lambda bi, si: (0, 0)),
                  pl.BlockSpec((d, 3 * d), lambda bi, si: (0, 0))],
        out_specs=[blk, blk, blk],
        out_shape=[jax.ShapeDtypeStruct((b, s, d), F32),
                   jax.ShapeDtypeStruct((b, s, d), BF16),
                   jax.ShapeDtypeStruct((b, s, d), BF16)],
        compiler_params=_cparams(2),
        name="qkv_proj",
    )(x, mod, norm_g.reshape(1, d), w_qkv.astype(BF16))


_SLOPE_LANE = 64
_BLKIDX_LANE = 67
_PAD_LANE = 70
PAD_BLOCKS = 2 * ATTN_GROUP
UNDERFLOW_LOG2 = -140.0
NORM_SLACK = 1.02


def _attn_kernel(slopes_ref, islopes_ref, q_ref, qnext_ref, k_ref, v_ref, o_ref,
                 kmean_ref, knorm_ref, kext_ref, v0_ref, v1_ref, lhs_ref, lhs_next_ref,
                 qnorm_ref, qnorm_next_ref, sa_ref, sb_ref, m_ref, acc_ref, *, nb):
    p = pl.program_id(1)
    i = pl.program_id(2)
    blk = MOBA_BLOCK
    pad = PAD_BLOCKS * blk
    lane = lax.broadcasted_iota(jnp.int32, (blk, LANES), 1)
    head0 = lane < HEAD_DIM

    nt = (((1,), (1,)), ((), ()))
    gw = ATTN_GROUP * blk
    lane2 = lax.broadcasted_iota(jnp.int32, (2 * blk, LANES), 1)
    top_half = lax.broadcasted_iota(jnp.int32, (2 * blk, LANES), 0) < blk

    def build_lhs(qblk, own, lhs_dst, qnorm_dst):
        q = qblk * (HEAD_DIM ** -0.5 * LOG2E)
        qs = jnp.concatenate([jnp.where(head0, q, 0.0), jnp.where(head0, 0.0, q)], axis=0)
        gate = lax.dot_general(kmean_ref[...], qs, nt, precision=HIGHEST,
                               preferred_element_type=F32)
        blk_i = lax.broadcasted_iota(jnp.int32, gate.shape, 0)
        blk_f = blk_i.astype(F32)
        valid = blk_i < own
        gate = jnp.where(valid, gate, -jnp.inf)
        sel = blk_i == own
        for _ in range(MOBA_TOPK):
            mx = jnp.max(gate, axis=0, keepdims=True)
            idx = jnp.min(jnp.where(gate == mx, blk_f, 1e9), axis=0, keepdims=True)
            hit = blk_f == idx
            sel = sel | (hit & valid)
            gate = jnp.where(hit, -jnp.inf, gate)
        ext = jnp.where(sel, 0.0, NEG_BIG).T
        slope = jnp.where(top_half, slopes_ref[2 * p], slopes_ref[2 * p + 1]) * LOG2E
        s_hi = slope.astype(BF16).astype(F32)
        rem = slope - s_hi
        s_mid = rem.astype(BF16).astype(F32)
        s_lo = rem - s_mid
        for k, part in enumerate((s_hi, s_mid, s_lo)):
            ext = jnp.where(lane2 == _SLOPE_LANE + k, part, ext)
            ext = jnp.where(lane2 == _BLKIDX_LANE + k, part * float(blk), ext)
        ext = jnp.where(lane2 > _PAD_LANE, 0.0, ext)
        lhs_dst[:, 0:LANES] = qs.astype(BF16)
        lhs_dst[:, LANES:2 * LANES] = ext.astype(BF16)
        qnorm_dst[...] = jnp.sqrt(jnp.sum(qs * qs, axis=-1, keepdims=True))

    @pl.when(i == 0)
    def _():
        kmean_ref[...] = jnp.zeros_like(kmean_ref)
        knorm_ref[...] = jnp.zeros_like(knorm_ref)
        kext_ref[0:pad, 0:LANES] = jnp.zeros((pad, LANES), BF16)
        lane_p = lax.broadcasted_iota(jnp.int32, (pad, LANES), 1)
        kext_ref[0:pad, LANES:2 * LANES] = jnp.where(lane_p == _PAD_LANE, 1.0, 0.0).astype(BF16)
        v0_ref[0:pad, :] = jnp.zeros((pad, LANES), BF16)
        v1_ref[0:pad, :] = jnp.zeros((pad, LANES), BF16)
        row = lax.broadcasted_iota(jnp.int32, (blk, LANES), 0).astype(F32)
        in_slope = (lane >= _SLOPE_LANE) & (lane < _SLOPE_LANE + 3)
        in_blk = (lane >= _BLKIDX_LANE) & (lane < _BLKIDX_LANE + 3)
        ones0 = jnp.where(lane == HEAD_DIM, 1.0, 0.0)
        ones1 = jnp.where(lane == 0, 1.0, 0.0)

        def build(j, carry):
            src = pl.ds(pl.multiple_of(j * blk, blk), blk)
            dst = pl.ds(pl.multiple_of(j * blk + pad, blk), blk)
            kj = k_ref[0, src, :]
            kf = kj.astype(F32)
            kmean_ref[pl.ds(j, 1), :] = jnp.mean(kf, axis=0, keepdims=True)
            ksq = kf * kf
            n0 = jnp.max(jnp.sum(jnp.where(head0, ksq, 0.0), axis=-1, keepdims=True),
                         axis=0, keepdims=True)
            n1 = jnp.max(jnp.sum(jnp.where(head0, 0.0, ksq), axis=-1, keepdims=True),
                         axis=0, keepdims=True)
            knorm_ref[0:1, :] = jnp.maximum(knorm_ref[0:1, :], n0)
            knorm_ref[1:2, :] = jnp.maximum(knorm_ref[1:2, :], n1)
            jf = jnp.full((blk, LANES), j, jnp.int32).astype(F32)
            extra = jnp.where(lane == j, 1.0, jnp.where(in_slope, row, jnp.where(in_blk, jf, 0.0)))
            kext_ref[dst, 0:LANES] = kj
            kext_ref[dst, LANES:2 * LANES] = extra.astype(BF16)
            vj = v_ref[0, src, :].astype(F32)
            v0_ref[dst, :] = jnp.where(head0, vj, ones0).astype(BF16)
            v1_ref[dst, :] = jnp.where(head0, ones1, vj).astype(BF16)
            return carry

        lax.fori_loop(0, nb, build, 0)

        build_lhs(q_ref[0], i, lhs_next_ref, qnorm_next_ref)

    lhs_ref[...] = lhs_next_ref[...]
    qnorm_ref[...] = qnorm_next_ref[...]
    m_ref[...] = jnp.full(m_ref.shape, NEG_BIG, F32)
    acc_ref[...] = jnp.zeros_like(acc_ref)

    def group_rows(t):
        start = (i - ATTN_GROUP * (t + 1) + 1 + PAD_BLOCKS) * blk
        return pl.ds(pl.multiple_of(start, blk), gw)

    def scores(t, dst_ref):
        dst_ref[...] = lax.dot_general(lhs_ref[...], kext_ref[group_rows(t), :], nt,
                                       preferred_element_type=F32)

    def softmax_group(t, src_ref, own_block=False):
        rows = group_rows(t)
        for hh in range(2):
            r = slice(hh * blk, (hh + 1) * blk)
            s = src_ref[r, :]
            if own_block:
                qi = lax.broadcasted_iota(jnp.int32, (blk, blk), 0)
                ki = lax.broadcasted_iota(jnp.int32, (blk, blk), 1)
                s = jnp.concatenate([s[:, :gw - blk],
                                     jnp.where(ki <= qi, s[:, gw - blk:], NEG_BIG)], axis=1)
            m_old = m_ref[r, :]
            m_new = jnp.maximum(m_old, jnp.max(s, axis=-1, keepdims=True))
            alpha = jnp.exp2(m_old - m_new)
            pm = jnp.exp2(s - m_new).astype(BF16)
            vref = v0_ref if hh == 0 else v1_ref
            pv = jnp.dot(pm, vref[rows, :], preferred_element_type=F32)
            acc_ref[r, :] = alpha * acc_ref[r, :] + pv
            m_ref[r, :] = m_new

    scores(0, sa_ref)
    scores(1, sb_ref)
    build_lhs(qnext_ref[0], i + 1, lhs_next_ref, qnorm_next_ref)
    softmax_group(0, sa_ref, own_block=True)

    knorm = jnp.sqrt(knorm_ref[0:2, 0:1])
    excess = (NORM_SLACK * qnorm_ref[...] * jnp.where(top_half[:, 0:1], knorm[0:1], knorm[1:2])
              - m_ref[...])
    a0 = jnp.max(excess[0:blk], axis=0, keepdims=True)
    a1 = jnp.max(excess[blk:2 * blk], axis=0, keepdims=True)
    edge = (blk - 1.0) / blk
    x0 = (UNDERFLOW_LOG2 - a0) * islopes_ref[2 * p] - edge
    x1 = (UNDERFLOW_LOG2 - a1) * islopes_ref[2 * p + 1] - edge
    i_f = jnp.full((1, 1), i, jnp.int32).astype(F32)
    first_needed = jnp.clip(jnp.floor(jnp.minimum(x0, x1)), 0.0, i_f)
    left = jnp.maximum(i_f - first_needed - (ATTN_GROUP - 1.0), 0.0)
    ngroups = jnp.ceil(left * (1.0 / ATTN_GROUP)).astype(jnp.int32)[0, 0]
    npair = ngroups // 2

    def body(u, carry):
        scores(2 * u + 2, sa_ref)
        softmax_group(2 * u + 1, sb_ref)
        scores(jnp.minimum(2 * u + 3, ngroups), sb_ref)
        softmax_group(2 * u + 2, sa_ref)
        return carry

    lax.fori_loop(0, npair, body, 0)

    @pl.when((ngroups & 1) == 1)
    def _():
        softmax_group(ngroups, sb_ref)

    a0 = acc_ref[0:blk, :]
    a1 = acc_ref[blk:2 * blk, :]
    l0 = a0[:, HEAD_DIM:HEAD_DIM + 1]
    l1 = a1[:, 0:1]
    o_ref[0] = jnp.where(head0, a0 / l0, a1 / l1).astype(o_ref.dtype)


def _attention(q, k, v):
    b, s, d = q.shape
    blk = MOBA_BLOCK
    nb = s // blk
    assert s % blk == 0 and nb <= HEAD_DIM, "block-select lanes hold at most 64 blocks"
    slopes = 2.0 ** (-8.0 * jnp.arange(1, N_HEADS + 1, dtype=F32) / N_HEADS)
    islopes = 1.0 / (slopes * (LOG2E * blk))
    pairs = d // LANES
    sp = s + PAD_BLOCKS * blk
    full = pl.BlockSpec((1, s, LANES), lambda bi, pi, ii: (bi, 0, pi))
    tile = pl.BlockSpec((1, blk, LANES), lambda bi, pi, ii: (bi, ii, pi))
    next_tile = pl.BlockSpec((1, blk, LANES), lambda bi, pi, ii: (bi, jnp.minimum(ii + 1, nb - 1), pi))
    smem = pl.BlockSpec(memory_space=pltpu.SMEM)
    return pl.pallas_call(
        functools.partial(_attn_kernel, nb=nb),
        grid=(b, pairs, nb),
        in_specs=[smem, smem, tile, next_tile, full, full],
        out_specs=tile,
        out_shape=jax.ShapeDtypeStruct((b, s, d), BF16),
        scratch_shapes=[pltpu.VMEM((LANES, LANES), F32),
                        pltpu.VMEM((8, LANES), F32),
                        pltpu.VMEM((sp, 2 * LANES), BF16),
                        pltpu.VMEM((sp, LANES), BF16),
                        pltpu.VMEM((sp, LANES), BF16),
                        pltpu.VMEM((2 * blk, 2 * LANES), BF16),
                        pltpu.VMEM((2 * blk, 2 * LANES), BF16),
                        pltpu.VMEM((2 * blk, 1), F32),
                        pltpu.VMEM((2 * blk, 1), F32),
                        pltpu.VMEM((2 * blk, ATTN_GROUP * blk), F32),
                        pltpu.VMEM((2 * blk, ATTN_GROUP * blk), F32),
                        pltpu.VMEM((2 * blk, 1), F32),
                        pltpu.VMEM((2 * blk, LANES), F32)],
        compiler_params=_cparams(3),
        name="moba_attn",
    )(slopes, islopes, q, q, k, v)


def _oproj_kernel(o_ref, x_ref, mod_ref, w_ref, out_ref):
    y = jnp.dot(o_ref[0], w_ref[...], preferred_element_type=F32)
    out_ref[0] = x_ref[0] + mod_ref[0][2:3] * y


def _oproj(o, x, mod, w_o):
    b, s, d = x.shape
    ts = 512
    blk = pl.BlockSpec((1, ts, d), lambda bi, si: (bi, si, 0))
    return pl.pallas_call(
        _oproj_kernel,
        grid=(b, s // ts),
        in_specs=[blk, blk,
                  pl.BlockSpec((1, 6, d), lambda bi, si: (bi, 0, 0)),
                  pl.BlockSpec((d, d), lambda bi, si: (0, 0))],
        out_specs=blk,
        out_shape=jax.ShapeDtypeStruct((b, s, d), F32),
        compiler_params=_cparams(2),
        name="attn_oproj",
    )(o, x, mod, w_o.astype(BF16))


def kernel(x, c, ada_w, ada_b, norm_mix_g, norm_ffn_g, pool_w, pool_scale, w_qkv, w_o,
           router_group_w, router_expert_w, exp_w_gate, exp_w_up, exp_w_down, final_norm_g):
    b, s, d = x.shape
    depth = ada_w.shape[0]
    mod = _adaln(c, ada_w, ada_b)
    for i in range(depth):
        if i % 2 == 0:
            x = _pool_layer(x, mod[i], norm_mix_g[i], pool_w[i // 2], pool_scale[i // 2])
        else:
            q, k, v = _qkv(x, mod[i], norm_mix_g[i], w_qkv[i // 2])
            o = _attention(q, k, v)
            x = _oproj(o, x, mod[i], w_o[i // 2])
        x = _moe_layer(x.reshape(b * s, d), mod[i], norm_ffn_g[i], router_group_w[i],
                       router_expert_w[i], exp_w_gate, exp_w_up, exp_w_down, i,
                       final_norm_g, s, final_norm=(i == depth - 1)).reshape(b, s, d)
    return x
```

```python
import functools

import jax
import jax.numpy as jnp
from jax import lax
from jax.experimental import pallas as pl
from jax.experimental.pallas import tpu as pltpu

F32 = jnp.float32
BF16 = jnp.bfloat16
HIGHEST = lax.Precision.HIGHEST

RMS_EPS = 1e-6
POOL_WINDOWS = (2, 4, 8, 16)
POOL_HALO = 16
N_HEADS = 16
HEAD_DIM = 64
MOBA_BLOCK = 256
MOBA_TOPK = 3
ATTN_GROUP = 4
N_EXPERT_GROUPS = 4
EXPERTS_PER_GROUP = 8
N_EXPERTS = N_EXPERT_GROUPS * EXPERTS_PER_GROUP
LANES = 128
NEG_BIG = -1e30
LOG2E = 1.4426950408889634
VMEM_LIMIT = 56 * 1024 * 1024


def _cparams(n_axes):
    return pltpu.CompilerParams(dimension_semantics=("arbitrary",) * n_axes,
                                vmem_limit_bytes=VMEM_LIMIT)


def _rms_mod(x, g, sc, sh):
    r = lax.rsqrt(jnp.mean(x * x, axis=-1, keepdims=True) + RMS_EPS)
    return (x * r) * g * (1.0 + sc) + sh


def _adaln_kernel(c_ref, w_ref, b_ref, o_ref):
    c = c_ref[...]
    ca = c / (1.0 + jnp.exp(-c))
    o_ref[0] = jnp.dot(ca, w_ref[0], precision=HIGHEST, preferred_element_type=F32) + b_ref[0]


def _adaln(c, ada_w, ada_b):
    depth, d, n = ada_w.shape
    b = c.shape[0]
    rows = 8
    cp = jnp.zeros((rows, d), F32).at[:b].set(c)
    tn = 1536
    out = pl.pallas_call(
        _adaln_kernel,
        grid=(depth, n // tn),
        in_specs=[pl.BlockSpec((rows, d), lambda i, j: (0, 0)),
                  pl.BlockSpec((1, d, tn), lambda i, j: (i, 0, j)),
                  pl.BlockSpec((1, 1, tn), lambda i, j: (i, 0, j))],
        out_specs=pl.BlockSpec((1, rows, tn), lambda i, j: (i, 0, j)),
        out_shape=jax.ShapeDtypeStruct((depth, rows, n), F32),
        compiler_params=_cparams(2),
        name="adaln",
    )(cp, ada_w, ada_b.reshape(depth, 1, n))
    return out[:, :b].reshape(depth, b, 6, d)


def _pool_kernel(x_ref, mod_ref, g_ref, w_ref, ps_ref, o_ref, hb_ref, *, ts):
    s = pl.program_id(1)

    @pl.when(s == 0)
    def _():
        hb_ref[0:POOL_HALO, :] = jnp.zeros((POOL_HALO, hb_ref.shape[1]), F32)

    x = x_ref[0]
    mod = mod_ref[0]
    h = _rms_mod(x, g_ref[...], mod[1:2], mod[0:1])
    hb_ref[POOL_HALO:POOL_HALO + ts, :] = h
    pos = s * ts + lax.broadcasted_iota(jnp.int32, (ts, 1), 0)
    gd = x.shape[1] // len(POOL_WINDOWS)
    ys = []
    for gi, w in enumerate(POOL_WINDOWS):
        c0 = gi * gd
        hg = h[:, c0:c0 + gd]
        acc = hg
        for k in range(1, w):
            acc = acc + hb_ref[POOL_HALO - k:POOL_HALO - k + ts, c0:c0 + gd]
        cnt = jnp.minimum(pos + 1, w).astype(F32)
        p = (acc / cnt - hg).astype(BF16)
        ys.append(jnp.dot(p, w_ref[gi], preferred_element_type=F32))
    y = jnp.concatenate(ys, axis=-1) * ps_ref[...]
    o_ref[0] = x + mod[2:3] * y
    hb_ref[0:POOL_HALO, :] = hb_ref[ts:ts + POOL_HALO, :]


def _pool_layer(x, mod, norm_g, pool_w, pool_scale):
    b, s, d = x.shape
    ts = 512
    g = len(POOL_WINDOWS)
    return pl.pallas_call(
        functools.partial(_pool_kernel, ts=ts),
        grid=(b, s // ts),
        in_specs=[pl.BlockSpec((1, ts, d), lambda bi, si: (bi, si, 0)),
                  pl.BlockSpec((1, 6, d), lambda bi, si: (bi, 0, 0)),
                  pl.BlockSpec((1, d), lambda bi, si: (0, 0)),
                  pl.BlockSpec((g, d // g, d // g), lambda bi, si: (0, 0, 0)),
                  pl.BlockSpec((1, d), lambda bi, si: (0, 0))],
        out_specs=pl.BlockSpec((1, ts, d), lambda bi, si: (bi, si, 0)),
        out_shape=jax.ShapeDtypeStruct((b, s, d), F32),
        scratch_shapes=[pltpu.VMEM((POOL_HALO + ts, d), F32)],
        compiler_params=_cparams(2),
        name="pool_mixer",
    )(x, mod, norm_g.reshape(1, d), pool_w.astype(BF16), pool_scale.reshape(1, d))


def _route(logits):
    lane = lax.broadcasted_iota(jnp.int32, logits.shape, 1).astype(F32)
    big = 1e9

    def first_argmax(v, vmax):
        return jnp.min(jnp.where(v == vmax, lane, big), axis=-1, keepdims=True)

    is_g = (lane >= N_EXPERTS) & (lane < N_EXPERTS + N_EXPERT_GROUPS)
    glog = jnp.where(is_g, logits, -jnp.inf)
    gmax = jnp.max(glog, axis=-1, keepdims=True)
    gsel = first_argmax(glog, gmax) - N_EXPERTS
    gsum = jnp.sum(jnp.exp(glog - gmax), axis=-1, keepdims=True)
    g_w = 1.0 / gsum
    lo = gsel * EXPERTS_PER_GROUP
    in_grp = (lane >= lo) & (lane < lo + EXPERTS_PER_GROUP)
    elog = jnp.where(in_grp, logits, -jnp.inf)
    m1 = jnp.max(elog, axis=-1, keepdims=True)
    i1 = first_argmax(elog, m1)
    elog2 = jnp.where(lane == i1, -jnp.inf, elog)
    m2 = jnp.max(elog2, axis=-1, keepdims=True)
    i2 = first_argmax(elog2, m2)
    d = jnp.exp(m2 - m1)
    w1 = g_w / (1.0 + d)
    w2 = g_w * d / (1.0 + d)
    return i1.astype(jnp.int32), i2.astype(jnp.int32), w1, w2


MOE_TILE = 512
ROUTE_TILE = 512
MOVE_TILE = 1024
_INFO_ROWS = 8
SUBLANES = 8


def _store_row_tiles(ref, v):
    groups = v.shape[1] // LANES
    for s in range(groups):
        ref[pl.ds(s, v.shape[0], stride=groups), :] = v[:, s * LANES:(s + 1) * LANES]


def _load_row_tiles(ref, groups):
    rows = ref.shape[0] // groups
    return jnp.concatenate([ref[pl.ds(s, rows, stride=groups), :] for s in range(groups)], axis=1)


def _row_tile(ref, r, groups):
    return ref.at[pl.ds(pl.multiple_of(r * groups, groups), groups), :]


def _route_kernel(x_ref, mod_ref, g_ref, wr_ref, hp_ref, info_ref, wts_ref, cnt_ref, carry_ref):
    t = pl.program_id(0)

    @pl.when(t == 0)
    def _():
        carry_ref[...] = jnp.zeros_like(carry_ref)

    mod = mod_ref[0]
    h = _rms_mod(x_ref[...], g_ref[...], mod[4:5], mod[3:4])
    _store_row_tiles(hp_ref, h)
    wr = wr_ref[...]
    h_hi = h.astype(BF16)
    h_lo = (h - h_hi.astype(F32)).astype(BF16)
    w_hi = wr.astype(BF16)
    w_lo = (wr - w_hi.astype(F32)).astype(BF16)
    logits = (jnp.dot(h_hi, w_hi, preferred_element_type=F32)
              + jnp.dot(h_lo, w_hi, preferred_element_type=F32)
              + jnp.dot(h_hi, w_lo, preferred_element_type=F32))
    i1, i2, w1, w2 = _route(logits)
    tm = logits.shape[0]
    lane = lax.broadcasted_iota(jnp.int32, logits.shape, 1)
    m1 = lane == i1
    m2 = lane == i2
    mask = jnp.where(m1 | m2, 1.0, 0.0)
    earlier = (lax.broadcasted_iota(jnp.int32, (tm, tm), 0)
               > lax.broadcasted_iota(jnp.int32, (tm, tm), 1))
    rank = jnp.dot(jnp.where(earlier, 1.0, 0.0).astype(BF16), mask.astype(BF16),
                   preferred_element_type=F32) + carry_ref[...]
    rank1 = jnp.sum(jnp.where(m1, rank, 0.0), axis=-1, keepdims=True)
    rank2 = jnp.sum(jnp.where(m2, rank, 0.0), axis=-1, keepdims=True)
    carry_ref[...] += jnp.sum(mask, axis=0, keepdims=True)
    cnt_ref[...] = carry_ref[...]
    cols = (i1.astype(F32), i2.astype(F32), rank1, rank2)
    z = jnp.zeros(logits.shape, F32)
    for c, col in enumerate(cols):
        z = jnp.where(lane == c, col, z)
    info_ref[0] = z.T[0:_INFO_ROWS, :].astype(jnp.int32)
    wts_ref[...] = jnp.where(lane == 0, w1, jnp.where(lane == 1, w2, 0.0))


def _row_positions(base_ref, info_ref, r):
    return (base_ref[info_ref[0, r]] + info_ref[2, r], base_ref[info_ref[1, r]] + info_ref[3, r])


def _load_info(info_hbm, info_ref, sem):
    cp = pltpu.make_async_copy(info_hbm.at[pl.program_id(0)], info_ref, sem)
    cp.start()
    cp.wait()


def _scatter_kernel(base_ref, ends_ref, info_hbm, hp_ref, hs_ref, info_ref, zero_ref, isem, sem,
                    *, n_experts, n_tiles):
    @pl.when(pl.program_id(0) == 0)
    def _():
        zero_ref[...] = jnp.zeros_like(zero_ref)

        def zero_tile(row0):
            dst = hs_ref.at[pl.ds(pl.multiple_of(row0 * SUBLANES, SUBLANES), MOE_TILE * SUBLANES), :]
            return pltpu.make_async_copy(zero_ref, dst, sem)

        def issue_e(e, issued):
            end = ends_ref[e]
            start = jnp.where(e == 0, 0, ends_ref[jnp.maximum(e - 1, 0)])
            nonempty = end > start

            @pl.when(nonempty)
            def _():
                zero_tile(pl.multiple_of(end - MOE_TILE, MOE_TILE)).start()

            return issued + nonempty.astype(jnp.int32)

        def issue_tail(s, carry):
            zero_tile(pl.multiple_of(s * MOE_TILE, MOE_TILE)).start()
            return carry

        first_tail = ends_ref[n_experts - 1] // MOE_TILE
        issued = lax.fori_loop(0, n_experts, issue_e, 0)
        lax.fori_loop(first_tail, n_tiles, issue_tail, 0)

        def drain_zero(s, carry):
            zero_tile(0).wait()
            return carry

        lax.fori_loop(0, issued + n_tiles - first_tail, drain_zero, 0)

    _load_info(info_hbm, info_ref, isem)

    def row_copy(r, pos):
        return pltpu.make_async_copy(_row_tile(hp_ref, r, SUBLANES), _row_tile(hs_ref, pos, SUBLANES),
                                     sem)

    def issue(r, carry):
        for slot, pos in enumerate(_row_positions(base_ref, info_ref, r)):
            row_copy(r, pos).start(priority=slot)
        return carry

    def drain(r, carry):
        row_copy(0, 0).wait()
        row_copy(0, 0).wait()
        return carry

    lax.fori_loop(0, MOVE_TILE, issue, 0, unroll=8)
    lax.fori_loop(0, MOVE_TILE, drain, 0, unroll=8)


def _expert_kernel(te_ref, hs_ref, wg_ref, wu_ref, wd_ref, ys_ref):
    del te_ref
    h = _load_row_tiles(hs_ref, SUBLANES).astype(BF16)
    gt = jnp.dot(h, wg_ref[0, 0].astype(BF16), preferred_element_type=F32)
    up = jnp.dot(h, wu_ref[0, 0].astype(BF16), preferred_element_type=F32)
    act = (gt / (1.0 + jnp.exp(-gt))) * up
    y = jnp.dot(act.astype(BF16), wd_ref[0, 0].astype(BF16), preferred_element_type=F32)
    _store_row_tiles(ys_ref, y)


def _combine_kernel(base_ref, info_hbm, x_ref, wts_ref, mod_ref, fg_ref, ys_ref, o_ref,
                    info_ref, y1_ref, y2_ref, isem, sem, *, final_norm):
    _load_info(info_hbm, info_ref, isem)

    def row_copy(pos, dst_ref, r):
        return pltpu.make_async_copy(_row_tile(ys_ref, pos, SUBLANES), _row_tile(dst_ref, r, SUBLANES),
                                     sem)

    def issue(r, carry):
        p1, p2 = _row_positions(base_ref, info_ref, r)
        row_copy(p1, y1_ref, r).start(priority=0)
        row_copy(p2, y2_ref, r).start(priority=1)
        return carry

    def drain(r, carry):
        row_copy(0, y1_ref, 0).wait()
        row_copy(0, y2_ref, 0).wait()
        return carry

    lax.fori_loop(0, MOVE_TILE, issue, 0, unroll=8)
    lax.fori_loop(0, MOVE_TILE, drain, 0, unroll=8)
    wts = wts_ref[...]
    y = (wts[:, 0:1] * _load_row_tiles(y1_ref, SUBLANES)
         + wts[:, 1:2] * _load_row_tiles(y2_ref, SUBLANES))
    xn = x_ref[...] + mod_ref[0][5:6] * y
    if final_norm:
        r = lax.rsqrt(jnp.mean(xn * xn, axis=-1, keepdims=True) + RMS_EPS)
        xn = (xn * r) * fg_ref[...]
    o_ref[...] = xn


def _moe_layer(x2, mod, norm_g, w_group, w_expert, w_gate, w_up, w_down, layer, final_g, seq,
               final_norm):
    t, d = x2.shape
    _, e, _, f = w_gate.shape
    assert d == SUBLANES * LANES, "a token row must fill exactly one (8, 128) tile"
    n_move = t // MOVE_TILE
    route_per_move = MOVE_TILE // ROUTE_TILE
    n_tiles = (2 * t) // MOE_TILE + e
    wr = jnp.zeros((d, LANES), F32).at[:, :e].set(w_expert).at[:, e:e + N_EXPERT_GROUPS].set(w_group)
    anyspace = pl.BlockSpec(memory_space=pl.ANY)

    hp, info, wts, cnt = pl.pallas_call(
        _route_kernel,
        grid=(t // ROUTE_TILE,),
        in_specs=[pl.BlockSpec((ROUTE_TILE, d), lambda ti: (ti, 0)),
                  pl.BlockSpec((1, 6, d), lambda ti: (ti * ROUTE_TILE // seq, 0, 0)),
                  pl.BlockSpec((1, d), lambda ti: (0, 0)),
                  pl.BlockSpec((d, LANES), lambda ti: (0, 0))],
        out_specs=[pl.BlockSpec((ROUTE_TILE * SUBLANES, LANES), lambda ti: (ti, 0)),
                   pl.BlockSpec((1, _INFO_ROWS, ROUTE_TILE),
                                lambda ti: (ti // route_per_move, 0, ti % route_per_move)),
                   pl.BlockSpec((ROUTE_TILE, LANES), lambda ti: (ti, 0)),
                   pl.BlockSpec((1, LANES), lambda ti: (0, 0))],
        out_shape=[jax.ShapeDtypeStruct((t * SUBLANES, LANES), F32),
                   jax.ShapeDtypeStruct((n_move, _INFO_ROWS, MOVE_TILE), jnp.int32),
                   jax.ShapeDtypeStruct((t, LANES), F32),
                   jax.ShapeDtypeStruct((1, LANES), F32)],
        scratch_shapes=[pltpu.VMEM((1, LANES), F32)],
        compiler_params=_cparams(1),
        name="moe_route",
    )(x2, mod, norm_g.reshape(1, d), wr)

    counts = cnt[0, :e].astype(jnp.int32)
    padded = (counts + MOE_TILE - 1) // MOE_TILE * MOE_TILE
    ends = jnp.cumsum(padded)
    base = ends - padded
    tile_start = jnp.arange(n_tiles, dtype=jnp.int32) * MOE_TILE
    tile_expert = jnp.minimum(jnp.sum((ends[None, :] <= tile_start[:, None]).astype(jnp.int32), axis=1),
                              e - 1)

    hs = pl.pallas_call(
        functools.partial(_scatter_kernel, n_experts=e, n_tiles=n_tiles),
        grid_spec=pltpu.PrefetchScalarGridSpec(
            num_scalar_prefetch=2,
            grid=(n_move,),
            in_specs=[anyspace,
                      pl.BlockSpec((MOVE_TILE * SUBLANES, LANES), lambda ti, b, en: (ti, 0))],
            out_specs=anyspace,
            scratch_shapes=[pltpu.SMEM((_INFO_ROWS, MOVE_TILE), jnp.int32),
                            pltpu.VMEM((MOE_TILE * SUBLANES, LANES), F32),
                            pltpu.SemaphoreType.DMA, pltpu.SemaphoreType.DMA]),
        out_shape=jax.ShapeDtypeStruct((n_tiles * MOE_TILE * SUBLANES, LANES), F32),
        compiler_params=_cparams(1),
        name="moe_scatter",
    )(base, ends, info, hp)

    ys = pl.pallas_call(
        _expert_kernel,
        grid_spec=pltpu.PrefetchScalarGridSpec(
            num_scalar_prefetch=1,
            grid=(n_tiles,),
            in_specs=[pl.BlockSpec((MOE_TILE * SUBLANES, LANES), lambda si, te: (si, 0)),
                      pl.BlockSpec((1, 1, d, f), lambda si, te: (layer, te[si], 0, 0)),
                      pl.BlockSpec((1, 1, d, f), lambda si, te: (layer, te[si], 0, 0)),
                      pl.BlockSpec((1, 1, f, d), lambda si, te: (layer, te[si], 0, 0))],
            out_specs=pl.BlockSpec((MOE_TILE * SUBLANES, LANES), lambda si, te: (si, 0))),
        out_shape=jax.ShapeDtypeStruct((n_tiles * MOE_TILE * SUBLANES, LANES), F32),
        compiler_params=_cparams(1),
        name="moe_experts",
    )(tile_expert, hs, w_gate, w_up, w_down)

    return pl.pallas_call(
        functools.partial(_combine_kernel, final_norm=final_norm),
        grid_spec=pltpu.PrefetchScalarGridSpec(
            num_scalar_prefetch=1,
            grid=(n_move,),
            in_specs=[anyspace,
                      pl.BlockSpec((MOVE_TILE, d), lambda ti, b: (ti, 0)),
                      pl.BlockSpec((MOVE_TILE, LANES), lambda ti, b: (ti, 0)),
                      pl.BlockSpec((1, 6, d), lambda ti, b: (ti * MOVE_TILE // seq, 0, 0)),
                      pl.BlockSpec((1, d), lambda ti, b: (0, 0)),
                      anyspace],
            out_specs=pl.BlockSpec((MOVE_TILE, d), lambda ti, b: (ti, 0)),
            scratch_shapes=[pltpu.SMEM((_INFO_ROWS, MOVE_TILE), jnp.int32),
                            pltpu.VMEM((MOVE_TILE * SUBLANES, LANES), F32),
                            pltpu.VMEM((MOVE_TILE * SUBLANES, LANES), F32),
                            pltpu.SemaphoreType.DMA, pltpu.SemaphoreType.DMA]),
        out_shape=jax.ShapeDtypeStruct((t, d), F32),
        compiler_params=_cparams(1),
        name="moe_combine_final" if final_norm else "moe_combine",
    )(base, info, x2, wts, mod, final_g.reshape(1, d), ys)


def _qkv_kernel(x_ref, mod_ref, g_ref, w_ref, q_ref, k_ref, v_ref):
    mod = mod_ref[0]
    h = _rms_mod(x_ref[0], g_ref[...], mod[1:2], mod[0:1]).astype(BF16)
    d = h.shape[1]
    q_ref[0] = jnp.dot(h, w_ref[:, 0:d], preferred_element_type=F32)
    k_ref[0] = jnp.dot(h, w_ref[:, d:2 * d], preferred_element_type=F32).astype(BF16)
    v_ref[0] = jnp.dot(h, w_ref[:, 2 * d:3 * d], preferred_element_type=F32).astype(BF16)


def _qkv(x, mod, norm_g, w_qkv):
    b, s, d = x.shape
    ts = 512
    blk = pl.BlockSpec((1, ts, d), lambda bi, si: (bi, si, 0))
    return pl.pallas_call(
        _qkv_kernel,
        grid=(b, s // ts),
        in_specs=[blk,
                  pl.BlockSpec((1, 6, d), lambda bi, si: (bi, 0, 0)),
                  pl.BlockSpec((1, d), lambda bi, si: (0, 0)),
                  pl.BlockSpec((d, 3 * d), lambda bi, si: (0, 0))],
        out_specs=[blk, blk, blk],
        out_shape=[jax.ShapeDtypeStruct((b, s, d), F32),
                   jax.ShapeDtypeStruct((b, s, d), BF16),
                   jax.ShapeDtypeStruct((b, s, d), BF16)],
        compiler_params=_cparams(2),
        name="qkv_proj",
    )(x, mod, norm_g.reshape(1, d), w_qkv.astype(BF16))


_SLOPE_LANE = 64
_BLKIDX_LANE = 67
_PAD_LANE = 70
PAD_BLOCKS = 2 * ATTN_GROUP
UNDERFLOW_LOG2 = -140.0
NORM_SLACK = 1.02


def _attn_kernel(slopes_ref, islopes_ref, q_ref, qnext_ref, k_ref, v_ref, o_ref,
                 kmean_ref, knorm_ref, kext_ref, v0_ref, v1_ref, lhs_ref, lhs_next_ref,
                 qnorm_ref, qnorm_next_ref, sa_ref, sb_ref, m_ref, acc_ref, *, nb):
    p = pl.program_id(1)
    i = pl.program_id(2)
    blk = MOBA_BLOCK
    pad = PAD_BLOCKS * blk
    lane = lax.broadcasted_iota(jnp.int32, (blk, LANES), 1)
    head0 = lane < HEAD_DIM

    nt = (((1,), (1,)), ((), ()))
    gw = ATTN_GROUP * blk
    lane2 = lax.broadcasted_iota(jnp.int32, (2 * blk, LANES), 1)
    top_half = lax.broadcasted_iota(jnp.int32, (2 * blk, LANES), 0) < blk

    def build_lhs(qblk, own, lhs_dst, qnorm_dst):
        q = qblk * (HEAD_DIM ** -0.5 * LOG2E)
        qs = jnp.concatenate([jnp.where(head0, q, 0.0), jnp.where(head0, 0.0, q)], axis=0)
        gate = lax.dot_general(kmean_ref[...], qs, nt, precision=HIGHEST,
                               preferred_element_type=F32)
        blk_i = lax.broadcasted_iota(jnp.int32, gate.shape, 0)
        blk_f = blk_i.astype(F32)
        valid = blk_i < own
        gate = jnp.where(valid, gate, -jnp.inf)
        sel = blk_i == own
        for _ in range(MOBA_TOPK):
            mx = jnp.max(gate, axis=0, keepdims=True)
            idx = jnp.min(jnp.where(gate == mx, blk_f, 1e9), axis=0, keepdims=True)
            hit = blk_f == idx
            sel = sel | (hit & valid)
            gate = jnp.where(hit, -jnp.inf, gate)
        ext = jnp.where(sel, 0.0, NEG_BIG).T
        slope = jnp.where(top_half, slopes_ref[2 * p], slopes_ref[2 * p + 1]) * LOG2E
        s_hi = slope.astype(BF16).astype(F32)
        rem = slope - s_hi
        s_mid = rem.astype(BF16).astype(F32)
        s_lo = rem - s_mid
        for k, part in enumerate((s_hi, s_mid, s_lo)):
            ext = jnp.where(lane2 == _SLOPE_LANE + k, part, ext)
            ext = jnp.where(lane2 == _BLKIDX_LANE + k, part * float(blk), ext)
        ext = jnp.where(lane2 > _PAD_LANE, 0.0, ext)
        lhs_dst[:, 0:LANES] = qs.astype(BF16)
        lhs_dst[:, LANES:2 * LANES] = ext.astype(BF16)
        qnorm_dst[...] = jnp.sqrt(jnp.sum(qs * qs, axis=-1, keepdims=True))

    @pl.when(i == 0)
    def _():
        kmean_ref[...] = jnp.zeros_like(kmean_ref)
        knorm_ref[...] = jnp.zeros_like(knorm_ref)
        kext_ref[0:pad, 0:LANES] = jnp.zeros((pad, LANES), BF16)
        lane_p = lax.broadcasted_iota(jnp.int32, (pad, LANES), 1)
        kext_ref[0:pad, LANES:2 * LANES] = jnp.where(lane_p == _PAD_LANE, 1.0, 0.0).astype(BF16)
        v0_ref[0:pad, :] = jnp.zeros((pad, LANES), BF16)
        v1_ref[0:pad, :] = jnp.zeros((pad, LANES), BF16)
        row = lax.broadcasted_iota(jnp.int32, (blk, LANES), 0).astype(F32)
        in_slope = (lane >= _SLOPE_LANE) & (lane < _SLOPE_LANE + 3)
        in_blk = (lane >= _BLKIDX_LANE) & (lane < _BLKIDX_LANE + 3)
        ones0 = jnp.where(lane == HEAD_DIM, 1.0, 0.0)
        ones1 = jnp.where(lane == 0, 1.0, 0.0)

        def build(j, carry):
            src = pl.ds(pl.multiple_of(j * blk, blk), blk)
            dst = pl.ds(pl.multiple_of(j * blk + pad, blk), blk)
            kj = k_ref[0, src, :]
            kf = kj.astype(F32)
            kmean_ref[pl.ds(j, 1), :] = jnp.mean(kf, axis=0, keepdims=True)
            ksq = kf * kf
            n0 = jnp.max(jnp.sum(jnp.where(head0, ksq, 0.0), axis=-1, keepdims=True),
                         axis=0, keepdims=True)
            n1 = jnp.max(jnp.sum(jnp.where(head0, 0.0, ksq), axis=-1, keepdims=True),
                         axis=0, keepdims=True)
            knorm_ref[0:1, :] = jnp.maximum(knorm_ref[0:1, :], n0)
            knorm_ref[1:2, :] = jnp.maximum(knorm_ref[1:2, :], n1)
            jf = jnp.full((blk, LANES), j, jnp.int32).astype(F32)
            extra = jnp.where(lane == j, 1.0, jnp.where(in_slope, row, jnp.where(in_blk, jf, 0.0)))
            kext_ref[dst, 0:LANES] = kj
            kext_ref[dst, LANES:2 * LANES] = extra.astype(BF16)
            vj = v_ref[0, src, :].astype(F32)
            v0_ref[dst, :] = jnp.where(head0, vj, ones0).astype(BF16)
            v1_ref[dst, :] = jnp.where(head0, ones1, vj).astype(BF16)
            return carry

        lax.fori_loop(0, nb, build, 0)

        build_lhs(q_ref[0], i, lhs_next_ref, qnorm_next_ref)

    lhs_ref[...] = lhs_next_ref[...]
    qnorm_ref[...] = qnorm_next_ref[...]
    m_ref[...] = jnp.full(m_ref.shape, NEG_BIG, F32)
    acc_ref[...] = jnp.zeros_like(acc_ref)

    def group_rows(t):
        start = (i - ATTN_GROUP * (t + 1) + 1 + PAD_BLOCKS) * blk
        return pl.ds(pl.multiple_of(start, blk), gw)

    def scores(t, dst_ref):
        dst_ref[...] = lax.dot_general(lhs_ref[...], kext_ref[group_rows(t), :], nt,
                                       preferred_element_type=F32)

    def softmax_group(t, src_ref, own_block=False):
        rows = group_rows(t)
        for hh in range(2):
            r = slice(hh * blk, (hh + 1) * blk)
            s = src_ref[r, :]
            if own_block:
                qi = lax.broadcasted_iota(jnp.int32, (blk, blk), 0)
                ki = lax.broadcasted_iota(jnp.int32, (blk, blk), 1)
                s = jnp.concatenate([s[:, :gw - blk],
                                     jnp.where(ki <= qi, s[:, gw - blk:], NEG_BIG)], axis=1)
            m_old = m_ref[r, :]
            m_new = jnp.maximum(m_old, jnp.max(s, axis=-1, keepdims=True))
            alpha = jnp.exp2(m_old - m_new)
            pm = jnp.exp2(s - m_new).astype(BF16)
            vref = v0_ref if hh == 0 else v1_ref
            pv = jnp.dot(pm, vref[rows, :], preferred_element_type=F32)
            acc_ref[r, :] = alpha * acc_ref[r, :] + pv
            m_ref[r, :] = m_new

    scores(0, sa_ref)
    scores(1, sb_ref)
    build_lhs(qnext_ref[0], i + 1, lhs_next_ref, qnorm_next_ref)
    softmax_group(0, sa_ref, own_block=True)

    knorm = jnp.sqrt(knorm_ref[0:2, 0:1])
    excess = (NORM_SLACK * qnorm_ref[...] * jnp.where(top_half[:, 0:1], knorm[0:1], knorm[1:2])
              - m_ref[...])
    a0 = jnp.max(excess[0:blk], axis=0, keepdims=True)
    a1 = jnp.max(excess[blk:2 * blk], axis=0, keepdims=True)
    edge = (blk - 1.0) / blk
    x0 = (UNDERFLOW_LOG2 - a0) * islopes_ref[2 * p] - edge
    x1 = (UNDERFLOW_LOG2 - a1) * islopes_ref[2 * p + 1] - edge
    i_f = jnp.full((1, 1), i, jnp.int32).astype(F32)
    first_needed = jnp.clip(jnp.floor(jnp.minimum(x0, x1)), 0.0, i_f)
    left = jnp.maximum(i_f - first_needed - (ATTN_GROUP - 1.0), 0.0)
    ngroups = jnp.ceil(left * (1.0 / ATTN_GROUP)).astype(jnp.int32)[0, 0]
    npair = ngroups // 2

    def body(u, carry):
        scores(2 * u + 2, sa_ref)
        softmax_group(2 * u + 1, sb_ref)
        scores(jnp.minimum(2 * u + 3, ngroups), sb_ref)
        softmax_group(2 * u + 2, sa_ref)
        return carry

    lax.fori_loop(0, npair, body, 0)

    @pl.when((ngroups & 1) == 1)
    def _():
        softmax_group(ngroups, sb_ref)

    a0 = acc_ref[0:blk, :]
    a1 = acc_ref[blk:2 * blk, :]
    l0 = a0[:, HEAD_DIM:HEAD_DIM + 1]
    l1 = a1[:, 0:1]
    o_ref[0] = jnp.where(head0, a0 / l0, a1 / l1).astype(o_ref.dtype)


def _attention(q, k, v):
    b, s, d = q.shape
    blk = MOBA_BLOCK
    nb = s // blk
    assert s % blk == 0 and nb <= HEAD_DIM, "block-select lanes hold at most 64 blocks"
    slopes = 2.0 ** (-8.0 * jnp.arange(1, N_HEADS + 1, dtype=F32) / N_HEADS)
    islopes = 1.0 / (slopes * (LOG2E * blk))
    pairs = d // LANES
    sp = s + PAD_BLOCKS * blk
    full = pl.BlockSpec((1, s, LANES), lambda bi, pi, ii: (bi, 0, pi))
    tile = pl.BlockSpec((1, blk, LANES), lambda bi, pi, ii: (bi, ii, pi))
    next_tile = pl.BlockSpec((1, blk, LANES), lambda bi, pi, ii: (bi, jnp.minimum(ii + 1, nb - 1), pi))
    smem = pl.BlockSpec(memory_space=pltpu.SMEM)
    return pl.pallas_call(
        functools.partial(_attn_kernel, nb=nb),
        grid=(b, pairs, nb),
        in_specs=[smem, smem, tile, next_tile, full, full],
        out_specs=tile,
        out_shape=jax.ShapeDtypeStruct((b, s, d), BF16),
        scratch_shapes=[pltpu.VMEM((LANES, LANES), F32),
                        pltpu.VMEM((8, LANES), F32),
                        pltpu.VMEM((sp, 2 * LANES), BF16),
                        pltpu.VMEM((sp, LANES), BF16),
                        pltpu.VMEM((sp, LANES), BF16),
                        pltpu.VMEM((2 * blk, 2 * LANES), BF16),
                        pltpu.VMEM((2 * blk, 2 * LANES), BF16),
                        pltpu.VMEM((2 * blk, 1), F32),
                        pltpu.VMEM((2 * blk, 1), F32),
                        pltpu.VMEM((2 * blk, ATTN_GROUP * blk), F32),
                        pltpu.VMEM((2 * blk, ATTN_GROUP * blk), F32),
                        pltpu.VMEM((2 * blk, 1), F32),
                        pltpu.VMEM((2 * blk, LANES), F32)],
        compiler_params=_cparams(3),
        name="moba_attn",
    )(slopes, islopes, q, q, k, v)


def _oproj_kernel(o_ref, x_ref, mod_ref, w_ref, out_ref):
    y = jnp.dot(o_ref[0], w_ref[...], preferred_element_type=F32)
    out_ref[0] = x_ref[0] + mod_ref[0][2:3] * y


def _oproj(o, x, mod, w_o):
    b, s, d = x.shape
    ts = 512
    blk = pl.BlockSpec((1, ts, d), lambda bi, si: (bi, si, 0))
    return pl.pallas_call(
        _oproj_kernel,
        grid=(b, s // ts),
        in_specs=[blk, blk,
                  pl.BlockSpec((1, 6, d), lambda bi, si: (bi, 0, 0)),
                  pl.BlockSpec((d, d), lambda bi, si: (0, 0))],
        out_specs=blk,
        out_shape=jax.ShapeDtypeStruct((b, s, d), F32),
        compiler_params=_cparams(2),
        name="attn_oproj",
    )(o, x, mod, w_o.astype(BF16))


def kernel(x, c, ada_w, ada_b, norm_mix_g, norm_ffn_g, pool_w, pool_scale, w_qkv, w_o,
           router_group_w, router_expert_w, exp_w_gate, exp_w_up, exp_w_down, final_norm_g):
    b, s, d = x.shape
    depth = ada_w.shape[0]
    mod = _adaln(c, ada_w, ada_b)
    for i in range(depth):
        if i % 2 == 0:
            x = _pool_layer(x, mod[i], norm_mix_g[i], pool_w[i // 2], pool_scale[i // 2])
        else:
            q, k, v = _qkv(x, mod[i], norm_mix_g[i], w_qkv[i // 2])
            o = _attention(q, k, v)
            x = _oproj(o, x, mod[i], w_o[i // 2])
        x = _moe_layer(x.reshape(b * s, d), mod[i], norm_ffn_g[i], router_group_w[i],
                       router_expert_w[i], exp_w_gate, exp_w_up, exp_w_down, i,
                       final_norm_g, s, final_norm=(i == depth - 1)).reshape(b, s, d)
    return x
```

```python
import functools

import jax
import jax.numpy as jnp
from jax import lax
from jax.experimental import pallas as pl
from jax.experimental.pallas import tpu as pltpu

F32 = jnp.float32
BF16 = jnp.bfloat16
HIGHEST = lax.Precision.HIGHEST

RMS_EPS = 1e-6
POOL_WINDOWS = (2, 4, 8, 16)
POOL_HALO = 16
N_HEADS = 16
HEAD_DIM = 64
MOBA_BLOCK = 256
MOBA_TOPK = 3
ATTN_GROUP = 4
N_EXPERT_GROUPS = 4
EXPERTS_PER_GROUP = 8
N_EXPERTS = N_EXPERT_GROUPS * EXPERTS_PER_GROUP
LANES = 128
NEG_BIG = -1e30
LOG2E = 1.4426950408889634
VMEM_LIMIT = 56 * 1024 * 1024


def _cparams(n_axes):
    return pltpu.CompilerParams(dimension_semantics=("arbitrary",) * n_axes,
                                vmem_limit_bytes=VMEM_LIMIT)


def _rms_mod(x, g, sc, sh):
    r = lax.rsqrt(jnp.mean(x * x, axis=-1, keepdims=True) + RMS_EPS)
    return (x * r) * g * (1.0 + sc) + sh


def _adaln_kernel(c_ref, w_ref, b_ref, o_ref):
    c = c_ref[...]
    ca = c / (1.0 + jnp.exp(-c))
    o_ref[0] = jnp.dot(ca, w_ref[0], precision=HIGHEST, preferred_element_type=F32) + b_ref[0]


def _adaln(c, ada_w, ada_b):
    depth, d, n = ada_w.shape
    b = c.shape[0]
    rows = 8
    cp = jnp.zeros((rows, d), F32).at[:b].set(c)
    tn = 1536
    out = pl.pallas_call(
        _adaln_kernel,
        grid=(depth, n // tn),
        in_specs=[pl.BlockSpec((rows, d), lambda i, j: (0, 0)),
                  pl.BlockSpec((1, d, tn), lambda i, j: (i, 0, j)),
                  pl.BlockSpec((1, 1, tn), lambda i, j: (i, 0, j))],
        out_specs=pl.BlockSpec((1, rows, tn), lambda i, j: (i, 0, j)),
        out_shape=jax.ShapeDtypeStruct((depth, rows, n), F32),
        compiler_params=_cparams(2),
        name="adaln",
    )(cp, ada_w, ada_b.reshape(depth, 1, n))
    return out[:, :b].reshape(depth, b, 6, d)


def _pool_kernel(x_ref, mod_ref, g_ref, w_ref, ps_ref, o_ref, hb_ref, *, ts):
    s = pl.program_id(1)

    @pl.when(s == 0)
    def _():
        hb_ref[0:POOL_HALO, :] = jnp.zeros((POOL_HALO, hb_ref.shape[1]), F32)

    x = x_ref[0]
    mod = mod_ref[0]
    h = _rms_mod(x, g_ref[...], mod[1:2], mod[0:1])
    hb_ref[POOL_HALO:POOL_HALO + ts, :] = h
    pos = s * ts + lax.broadcasted_iota(jnp.int32, (ts, 1), 0)
    gd = x.shape[1] // len(POOL_WINDOWS)
    ys = []
    for gi, w in enumerate(POOL_WINDOWS):
        c0 = gi * gd
        hg = h[:, c0:c0 + gd]
        acc = hg
        for k in range(1, w):
            acc = acc + hb_ref[POOL_HALO - k:POOL_HALO - k + ts, c0:c0 + gd]
        cnt = jnp.minimum(pos + 1, w).astype(F32)
        p = (acc / cnt - hg).astype(BF16)
        ys.append(jnp.dot(p, w_ref[gi], preferred_element_type=F32))
    y = jnp.concatenate(ys, axis=-1) * ps_ref[...]
    o_ref[0] = x + mod[2:3] * y
    hb_ref[0:POOL_HALO, :] = hb_ref[ts:ts + POOL_HALO, :]


def _pool_layer(x, mod, norm_g, pool_w, pool_scale):
    b, s, d = x.shape
    ts = 512
    g = len(POOL_WINDOWS)
    return pl.pallas_call(
        functools.partial(_pool_kernel, ts=ts),
        grid=(b, s // ts),
        in_specs=[pl.BlockSpec((1, ts, d), lambda bi, si: (bi, si, 0)),
                  pl.BlockSpec((1, 6, d), lambda bi, si: (bi, 0, 0)),
                  pl.BlockSpec((1, d), lambda bi, si: (0, 0)),
                  pl.BlockSpec((g, d // g, d // g), lambda bi, si: (0, 0, 0)),
                  pl.BlockSpec((1, d), lambda bi, si: (0, 0))],
        out_specs=pl.BlockSpec((1, ts, d), lambda bi, si: (bi, si, 0)),
        out_shape=jax.ShapeDtypeStruct((b, s, d), F32),
        scratch_shapes=[pltpu.VMEM((POOL_HALO + ts, d), F32)],
        compiler_params=_cparams(2),
        name="pool_mixer",
    )(x, mod, norm_g.reshape(1, d), pool_w.astype(BF16), pool_scale.reshape(1, d))


def _route(logits):
    lane = lax.broadcasted_iota(jnp.int32, logits.shape, 1).astype(F32)
    big = 1e9

    def first_argmax(v, vmax):
        return jnp.min(jnp.where(v == vmax, lane, big), axis=-1, keepdims=True)

    is_g = (lane >= N_EXPERTS) & (lane < N_EXPERTS + N_EXPERT_GROUPS)
    glog = jnp.where(is_g, logits, -jnp.inf)
    gmax = jnp.max(glog, axis=-1, keepdims=True)
    gsel = first_argmax(glog, gmax) - N_EXPERTS
    gsum = jnp.sum(jnp.exp(glog - gmax), axis=-1, keepdims=True)
    g_w = 1.0 / gsum
    lo = gsel * EXPERTS_PER_GROUP
    in_grp = (lane >= lo) & (lane < lo + EXPERTS_PER_GROUP)
    elog = jnp.where(in_grp, logits, -jnp.inf)
    m1 = jnp.max(elog, axis=-1, keepdims=True)
    i1 = first_argmax(elog, m1)
    elog2 = jnp.where(lane == i1, -jnp.inf, elog)
    m2 = jnp.max(elog2, axis=-1, keepdims=True)
    i2 = first_argmax(elog2, m2)
    d = jnp.exp(m2 - m1)
    w1 = g_w / (1.0 + d)
    w2 = g_w * d / (1.0 + d)
    return i1.astype(jnp.int32), i2.astype(jnp.int32), w1, w2


MOE_TILE = 512
ROUTE_TILE = 512
MOVE_TILE = 1024
_INFO_ROWS = 8
SUBLANES = 8


def _store_row_tiles(ref, v):
    groups = v.shape[1] // LANES
    for s in range(groups):
        ref[pl.ds(s, v.shape[0], stride=groups), :] = v[:, s * LANES:(s + 1) * LANES]


def _load_row_tiles(ref, groups):
    rows = ref.shape[0] // groups
    return jnp.concatenate([ref[pl.ds(s, rows, stride=groups), :] for s in range(groups)], axis=1)


def _row_tile(ref, r, groups):
    return ref.at[pl.ds(pl.multiple_of(r * groups, groups), groups), :]


def _route_kernel(x_ref, mod_ref, g_ref, wr_ref, hp_ref, info_ref, wts_ref, cnt_ref, carry_ref):
    t = pl.program_id(0)

    @pl.when(t == 0)
    def _():
        carry_ref[...] = jnp.zeros_like(carry_ref)

    mod = mod_ref[0]
    h = _rms_mod(x_ref[...], g_ref[...], mod[4:5], mod[3:4])
    _store_row_tiles(hp_ref, h)
    wr = wr_ref[...]
    h_hi = h.astype(BF16)
    h_lo = (h - h_hi.astype(F32)).astype(BF16)
    w_hi = wr.astype(BF16)
    w_lo = (wr - w_hi.astype(F32)).astype(BF16)
    logits = (jnp.dot(h_hi, w_hi, preferred_element_type=F32)
              + jnp.dot(h_lo, w_hi, preferred_element_type=F32)
              + jnp.dot(h_hi, w_lo, preferred_element_type=F32))
    i1, i2, w1, w2 = _route(logits)
    tm = logits.shape[0]
    lane = lax.broadcasted_iota(jnp.int32, logits.shape, 1)
    m1 = lane == i1
    m2 = lane == i2
    mask = jnp.where(m1 | m2, 1.0, 0.0)
    earlier = (lax.broadcasted_iota(jnp.int32, (tm, tm), 0)
               > lax.broadcasted_iota(jnp.int32, (tm, tm), 1))
    rank = jnp.dot(jnp.where(earlier, 1.0, 0.0).astype(BF16), mask.astype(BF16),
                   preferred_element_type=F32) + carry_ref[...]
    rank1 = jnp.sum(jnp.where(m1, rank, 0.0), axis=-1, keepdims=True)
    rank2 = jnp.sum(jnp.where(m2, rank, 0.0), axis=-1, keepdims=True)
    carry_ref[...] += jnp.sum(mask, axis=0, keepdims=True)
    cnt_ref[...] = carry_ref[...]
    cols = (i1.astype(F32), i2.astype(F32), rank1, rank2)
    z = jnp.zeros(logits.shape, F32)
    for c, col in enumerate(cols):
        z = jnp.where(lane == c, col, z)
    info_ref[0] = z.T[0:_INFO_ROWS, :].astype(jnp.int32)
    wts_ref[...] = jnp.where(lane == 0, w1, jnp.where(lane == 1, w2, 0.0))


def _row_positions(base_ref, info_ref, r):
    return (base_ref[info_ref[0, r]] + info_ref[2, r], base_ref[info_ref[1, r]] + info_ref[3, r])


def _load_info(info_hbm, info_ref, sem):
    cp = pltpu.make_async_copy(info_hbm.at[pl.program_id(0)], info_ref, sem)
    cp.start()
    cp.wait()


def _scatter_kernel(base_ref, ends_ref, info_hbm, hp_ref, hs_ref, info_ref, zero_ref, isem, sem,
                    *, n_experts, n_tiles):
    @pl.when(pl.program_id(0) == 0)
    def _():
        zero_ref[...] = jnp.zeros_like(zero_ref)

        def zero_tile(row0):
            dst = hs_ref.at[pl.ds(pl.multiple_of(row0 * SUBLANES, SUBLANES), MOE_TILE * SUBLANES), :]
            return pltpu.make_async_copy(zero_ref, dst, sem)

        def issue_e(e, issued):
            end = ends_ref[e]
            start = jnp.where(e == 0, 0, ends_ref[jnp.maximum(e - 1, 0)])
            nonempty = end > start

            @pl.when(nonempty)
            def _():
                zero_tile(pl.multiple_of(end - MOE_TILE, MOE_TILE)).start()

            return issued + nonempty.astype(jnp.int32)

        def issue_tail(s, carry):
            zero_tile(pl.multiple_of(s * MOE_TILE, MOE_TILE)).start()
            return carry

        first_tail = ends_ref[n_experts - 1] // MOE_TILE
        issued = lax.fori_loop(0, n_experts, issue_e, 0)
        lax.fori_loop(first_tail, n_tiles, issue_tail, 0)

        def drain_zero(s, carry):
            zero_tile(0).wait()
            return carry

        lax.fori_loop(0, issued + n_tiles - first_tail, drain_zero, 0)

    _load_info(info_hbm, info_ref, isem)

    def row_copy(r, pos):
        return pltpu.make_async_copy(_row_tile(hp_ref, r, SUBLANES), _row_tile(hs_ref, pos, SUBLANES),
                                     sem)

    def issue(r, carry):
        for pos in _row_positions(base_ref, info_ref, r):
            row_copy(r, pos).start()
        return carry

    def drain(r, carry):
        row_copy(0, 0).wait()
        row_copy(0, 0).wait()
        return carry

    lax.fori_loop(0, MOVE_TILE, issue, 0, unroll=8)
    lax.fori_loop(0, MOVE_TILE, drain, 0, unroll=8)


def _expert_kernel(te_ref, hs_ref, wg_ref, wu_ref, wd_ref, ys_ref):
    del te_ref
    h = _load_row_tiles(hs_ref, SUBLANES).astype(BF16)
    gt = jnp.dot(h, wg_ref[0, 0].astype(BF16), preferred_element_type=F32)
    up = jnp.dot(h, wu_ref[0, 0].astype(BF16), preferred_element_type=F32)
    act = (gt / (1.0 + jnp.exp(-gt))) * up
    y = jnp.dot(act.astype(BF16), wd_ref[0, 0].astype(BF16), preferred_element_type=F32)
    _store_row_tiles(ys_ref, y)


def _combine_kernel(base_ref, info_hbm, x_ref, wts_ref, mod_ref, fg_ref, ys_ref, o_ref,
                    info_ref, y1_ref, y2_ref, isem, sem, *, final_norm):
    _load_info(info_hbm, info_ref, isem)

    def row_copy(pos, dst_ref, r):
        return pltpu.make_async_copy(_row_tile(ys_ref, pos, SUBLANES), _row_tile(dst_ref, r, SUBLANES),
                                     sem)

    def issue(r, carry):
        p1, p2 = _row_positions(base_ref, info_ref, r)
        row_copy(p1, y1_ref, r).start()
        row_copy(p2, y2_ref, r).start()
        return carry

    def drain(r, carry):
        row_copy(0, y1_ref, 0).wait()
        row_copy(0, y2_ref, 0).wait()
        return carry

    lax.fori_loop(0, MOVE_TILE, issue, 0, unroll=8)
    lax.fori_loop(0, MOVE_TILE, drain, 0, unroll=8)
    wts = wts_ref[...]
    y = (wts[:, 0:1] * _load_row_tiles(y1_ref, SUBLANES)
         + wts[:, 1:2] * _load_row_tiles(y2_ref, SUBLANES))
    xn = x_ref[...] + mod_ref[0][5:6] * y
    if final_norm:
        r = lax.rsqrt(jnp.mean(xn * xn, axis=-1, keepdims=True) + RMS_EPS)
        xn = (xn * r) * fg_ref[...]
    o_ref[...] = xn


def _moe_layer(x2, mod, norm_g, w_group, w_expert, w_gate, w_up, w_down, layer, final_g, seq,
               final_norm):
    t, d = x2.shape
    _, e, _, f = w_gate.shape
    assert d == SUBLANES * LANES, "a token row must fill exactly one (8, 128) tile"
    n_move = t // MOVE_TILE
    route_per_move = MOVE_TILE // ROUTE_TILE
    n_tiles = (2 * t) // MOE_TILE + e
    wr = jnp.zeros((d, LANES), F32).at[:, :e].set(w_expert).at[:, e:e + N_EXPERT_GROUPS].set(w_group)
    anyspace = pl.BlockSpec(memory_space=pl.ANY)

    hp, info, wts, cnt = pl.pallas_call(
        _route_kernel,
        grid=(t // ROUTE_TILE,),
        in_specs=[pl.BlockSpec((ROUTE_TILE, d), lambda ti: (ti, 0)),
                  pl.BlockSpec((1, 6, d), lambda ti: (ti * ROUTE_TILE // seq, 0, 0)),
                  pl.BlockSpec((1, d), lambda ti: (0, 0)),
                  pl.BlockSpec((d, LANES), lambda ti: (0, 0))],
        out_specs=[pl.BlockSpec((ROUTE_TILE * SUBLANES, LANES), lambda ti: (ti, 0)),
                   pl.BlockSpec((1, _INFO_ROWS, ROUTE_TILE),
                                lambda ti: (ti // route_per_move, 0, ti % route_per_move)),
                   pl.BlockSpec((ROUTE_TILE, LANES), lambda ti: (ti, 0)),
                   pl.BlockSpec((1, LANES), lambda ti: (0, 0))],
        out_shape=[jax.ShapeDtypeStruct((t * SUBLANES, LANES), F32),
                   jax.ShapeDtypeStruct((n_move, _INFO_ROWS, MOVE_TILE), jnp.int32),
                   jax.ShapeDtypeStruct((t, LANES), F32),
                   jax.ShapeDtypeStruct((1, LANES), F32)],
        scratch_shapes=[pltpu.VMEM((1, LANES), F32)],
        compiler_params=_cparams(1),
        name="moe_route",
    )(x2, mod, norm_g.reshape(1, d), wr)

    counts = cnt[0, :e].astype(jnp.int32)
    padded = (counts + MOE_TILE - 1) // MOE_TILE * MOE_TILE
    ends = jnp.cumsum(padded)
    base = ends - padded
    tile_start = jnp.arange(n_tiles, dtype=jnp.int32) * MOE_TILE
    tile_expert = jnp.minimum(jnp.sum((ends[None, :] <= tile_start[:, None]).astype(jnp.int32), axis=1),
                              e - 1)

    hs = pl.pallas_call(
        functools.partial(_scatter_kernel, n_experts=e, n_tiles=n_tiles),
        grid_spec=pltpu.PrefetchScalarGridSpec(
            num_scalar_prefetch=2,
            grid=(n_move,),
            in_specs=[anyspace,
                      pl.BlockSpec((MOVE_TILE * SUBLANES, LANES), lambda ti, b, en: (ti, 0))],
            out_specs=anyspace,
            scratch_shapes=[pltpu.SMEM((_INFO_ROWS, MOVE_TILE), jnp.int32),
                            pltpu.VMEM((MOE_TILE * SUBLANES, LANES), F32),
                            pltpu.SemaphoreType.DMA, pltpu.SemaphoreType.DMA]),
        out_shape=jax.ShapeDtypeStruct((n_tiles * MOE_TILE * SUBLANES, LANES), F32),
        compiler_params=_cparams(1),
        name="moe_scatter",
    )(base, ends, info, hp)

    ys = pl.pallas_call(
        _expert_kernel,
        grid_spec=pltpu.PrefetchScalarGridSpec(
            num_scalar_prefetch=1,
            grid=(n_tiles,),
            in_specs=[pl.BlockSpec((MOE_TILE * SUBLANES, LANES), lambda si, te: (si, 0)),
                      pl.BlockSpec((1, 1, d, f), lambda si, te: (layer, te[si], 0, 0)),
                      pl.BlockSpec((1, 1, d, f), lambda si, te: (layer, te[si], 0, 0)),
                      pl.BlockSpec((1, 1, f, d), lambda si, te: (layer, te[si], 0, 0))],
            out_specs=pl.BlockSpec((MOE_TILE * SUBLANES, LANES), lambda si, te: (si, 0))),
        out_shape=jax.ShapeDtypeStruct((n_tiles * MOE_TILE * SUBLANES, LANES), F32),
        compiler_params=_cparams(1),
        name="moe_experts",
    )(tile_expert, hs, w_gate, w_up, w_down)

    return pl.pallas_call(
        functools.partial(_combine_kernel, final_norm=final_norm),
        grid_spec=pltpu.PrefetchScalarGridSpec(
            num_scalar_prefetch=1,
            grid=(n_move,),
            in_specs=[anyspace,
                      pl.BlockSpec((MOVE_TILE, d), lambda ti, b: (ti, 0)),
                      pl.BlockSpec((MOVE_TILE, LANES), lambda ti, b: (ti, 0)),
                      pl.BlockSpec((1, 6, d), lambda ti, b: (ti * MOVE_TILE // seq, 0, 0)),
                      pl.BlockSpec((1, d), lambda ti, b: (0, 0)),
                      anyspace],
            out_specs=pl.BlockSpec((MOVE_TILE, d), lambda ti, b: (ti, 0)),
            scratch_shapes=[pltpu.SMEM((_INFO_ROWS, MOVE_TILE), jnp.int32),
                            pltpu.VMEM((MOVE_TILE * SUBLANES, LANES), F32),
                            pltpu.VMEM((MOVE_TILE * SUBLANES, LANES), F32),
                            pltpu.SemaphoreType.DMA, pltpu.SemaphoreType.DMA]),
        out_shape=jax.ShapeDtypeStruct((t, d), F32),
        compiler_params=_cparams(1),
        name="moe_combine_final" if final_norm else "moe_combine",
    )(base, info, x2, wts, mod, final_g.reshape(1, d), ys)


def _qkv_kernel(x_ref, mod_ref, g_ref, w_ref, q_ref, k_ref, v_ref):
    mod = mod_ref[0]
    h = _rms_mod(x_ref[0], g_ref[...], mod[1:2], mod[0:1]).astype(BF16)
    d = h.shape[1]
    q_ref[0] = jnp.dot(h, w_ref[:, 0:d], preferred_element_type=F32)
    k_ref[0] = jnp.dot(h, w_ref[:, d:2 * d], preferred_element_type=F32).astype(BF16)
    v_ref[0] = jnp.dot(h, w_ref[:, 2 * d:3 * d], preferred_element_type=F32).astype(BF16)


def _qkv(x, mod, norm_g, w_qkv):
    b, s, d = x.shape
    ts = 512
    blk = pl.BlockSpec((1, ts, d), lambda bi, si: (bi, si, 0))
    return pl.pallas_call(
        _qkv_kernel,
        grid=(b, s // ts),
        in_specs=[blk,
                  pl.BlockSpec((1, 6, d), lambda bi, si: (bi, 0, 0)),
                  pl.BlockSpec((1, d), lambda bi, si: (0, 0)),
                  pl.BlockSpec((d, 3 * d), lambda bi, si: (0, 0))],
        out_specs=[blk, blk, blk],
        out_shape=[jax.ShapeDtypeStruct((b, s, d), F32),
                   jax.ShapeDtypeStruct((b, s, d), BF16),
                   jax.ShapeDtypeStruct((b, s, d), BF16)],
        compiler_params=_cparams(2),
        name="qkv_proj",
    )(x, mod, norm_g.reshape(1, d), w_qkv.astype(BF16))


_SLOPE_LANE = 64
_BLKIDX_LANE = 67
_PAD_LANE = 70
PAD_BLOCKS = 2 * ATTN_GROUP
UNDERFLOW_LOG2 = -140.0
NORM_SLACK = 1.02


def _attn_kernel(slopes_ref, islopes_ref, q_ref, qnext_ref, k_ref, v_ref, o_ref,
                 kmean_ref, knorm_ref, kext_ref, v0_ref, v1_ref, lhs_ref, lhs_next_ref,
                 qnorm_ref, qnorm_next_ref, sa_ref, sb_ref, m_ref, acc_ref, *, nb):
    p = pl.program_id(1)
    i = pl.program_id(2)
    blk = MOBA_BLOCK
    pad = PAD_BLOCKS * blk
    lane = lax.broadcasted_iota(jnp.int32, (blk, LANES), 1)
    head0 = lane < HEAD_DIM

    nt = (((1,), (1,)), ((), ()))
    gw = ATTN_GROUP * blk
    lane2 = lax.broadcasted_iota(jnp.int32, (2 * blk, LANES), 1)
    top_half = lax.broadcasted_iota(jnp.int32, (2 * blk, LANES), 0) < blk

    def build_lhs(qblk, own, lhs_dst, qnorm_dst):
        q = qblk * (HEAD_DIM ** -0.5 * LOG2E)
        qs = jnp.concatenate([jnp.where(head0, q, 0.0), jnp.where(head0, 0.0, q)], axis=0)
        gate = lax.dot_general(kmean_ref[0:HEAD_DIM, :], qs, nt, precision=HIGHEST,
                               preferred_element_type=F32)
        blk_i = lax.broadcasted_iota(jnp.int32, gate.shape, 0)
        blk_f = blk_i.astype(F32)
        valid = blk_i < own
        gate = jnp.where(valid, gate, -jnp.inf)
        sel = blk_i == own
        for _ in range(MOBA_TOPK):
            mx = jnp.max(gate, axis=0, keepdims=True)
            idx = jnp.min(jnp.where(gate == mx, blk_f, 1e9), axis=0, keepdims=True)
            hit = blk_f == idx
            sel = sel | (hit & valid)
            gate = jnp.where(hit, -jnp.inf, gate)
        bias = jnp.where(sel, 0.0, NEG_BIG)
        ext = jnp.concatenate([bias, jnp.full(bias.shape, NEG_BIG, F32)], axis=0).T
        slope = jnp.where(top_half, slopes_ref[2 * p], slopes_ref[2 * p + 1]) * LOG2E
        s_hi = slope.astype(BF16).astype(F32)
        rem = slope - s_hi
        s_mid = rem.astype(BF16).astype(F32)
        s_lo = rem - s_mid
        for k, part in enumerate((s_hi, s_mid, s_lo)):
            ext = jnp.where(lane2 == _SLOPE_LANE + k, part, ext)
            ext = jnp.where(lane2 == _BLKIDX_LANE + k, part * float(blk), ext)
        ext = jnp.where(lane2 > _PAD_LANE, 0.0, ext)
        lhs_dst[:, 0:LANES] = qs.astype(BF16)
        lhs_dst[:, LANES:2 * LANES] = ext.astype(BF16)
        qnorm_dst[...] = jnp.sqrt(jnp.sum(qs * qs, axis=-1, keepdims=True))

    @pl.when(i == 0)
    def _():
        kmean_ref[...] = jnp.zeros_like(kmean_ref)
        knorm_ref[...] = jnp.zeros_like(knorm_ref)
        kext_ref[0:pad, 0:LANES] = jnp.zeros((pad, LANES), BF16)
        lane_p = lax.broadcasted_iota(jnp.int32, (pad, LANES), 1)
        kext_ref[0:pad, LANES:2 * LANES] = jnp.where(lane_p == _PAD_LANE, 1.0, 0.0).astype(BF16)
        v0_ref[0:pad, :] = jnp.zeros((pad, LANES), BF16)
        v1_ref[0:pad, :] = jnp.zeros((pad, LANES), BF16)
        row = lax.broadcasted_iota(jnp.int32, (blk, LANES), 0).astype(F32)
        in_slope = (lane >= _SLOPE_LANE) & (lane < _SLOPE_LANE + 3)
        in_blk = (lane >= _BLKIDX_LANE) & (lane < _BLKIDX_LANE + 3)
        ones0 = jnp.where(lane == HEAD_DIM, 1.0, 0.0)
        ones1 = jnp.where(lane == 0, 1.0, 0.0)

        def build(j, carry):
            src = pl.ds(pl.multiple_of(j * blk, blk), blk)
            dst = pl.ds(pl.multiple_of(j * blk + pad, blk), blk)
            kj = k_ref[0, src, :]
            kf = kj.astype(F32)
            kmean_ref[pl.ds(j, 1), :] = jnp.mean(kf, axis=0, keepdims=True)
            ksq = kf * kf
            n0 = jnp.max(jnp.sum(jnp.where(head0, ksq, 0.0), axis=-1, keepdims=True),
                         axis=0, keepdims=True)
            n1 = jnp.max(jnp.sum(jnp.where(head0, 0.0, ksq), axis=-1, keepdims=True),
                         axis=0, keepdims=True)
            knorm_ref[0:1, :] = jnp.maximum(knorm_ref[0:1, :], n0)
            knorm_ref[1:2, :] = jnp.maximum(knorm_ref[1:2, :], n1)
            jf = jnp.full((blk, LANES), j, jnp.int32).astype(F32)
            extra = jnp.where(lane == j, 1.0, jnp.where(in_slope, row, jnp.where(in_blk, jf, 0.0)))
            kext_ref[dst, 0:LANES] = kj
            kext_ref[dst, LANES:2 * LANES] = extra.astype(BF16)
            vj = v_ref[0, src, :].astype(F32)
            v0_ref[dst, :] = jnp.where(head0, vj, ones0).astype(BF16)
            v1_ref[dst, :] = jnp.where(head0, ones1, vj).astype(BF16)
            return carry

        lax.fori_loop(0, nb, build, 0)

        build_lhs(q_ref[0], i, lhs_next_ref, qnorm_next_ref)

    lhs_ref[...] = lhs_next_ref[...]
    qnorm_ref[...] = qnorm_next_ref[...]
    m_ref[...] = jnp.full(m_ref.shape, NEG_BIG, F32)
    acc_ref[...] = jnp.zeros_like(acc_ref)

    def group_rows(t):
        start = (i - ATTN_GROUP * (t + 1) + 1 + PAD_BLOCKS) * blk
        return pl.ds(pl.multiple_of(start, blk), gw)

    def scores(t, dst_ref):
        dst_ref[...] = lax.dot_general(lhs_ref[...], kext_ref[group_rows(t), :], nt,
                                       preferred_element_type=F32)

    def softmax_group(t, src_ref, own_block=False):
        rows = group_rows(t)
        for hh in range(2):
            r = slice(hh * blk, (hh + 1) * blk)
            s = src_ref[r, :]
            if own_block:
                qi = lax.broadcasted_iota(jnp.int32, (blk, blk), 0)
                ki = lax.broadcasted_iota(jnp.int32, (blk, blk), 1)
                s = jnp.concatenate([s[:, :gw - blk],
                                     jnp.where(ki <= qi, s[:, gw - blk:], NEG_BIG)], axis=1)
            m_old = m_ref[r, :]
            m_new = jnp.maximum(m_old, jnp.max(s, axis=-1, keepdims=True))
            alpha = jnp.exp2(m_old - m_new)
            pm = jnp.exp2(s - m_new).astype(BF16)
            vref = v0_ref if hh == 0 else v1_ref
            pv = jnp.dot(pm, vref[rows, :], preferred_element_type=F32)
            acc_ref[r, :] = alpha * acc_ref[r, :] + pv
            m_ref[r, :] = m_new

    scores(0, sa_ref)
    scores(1, sb_ref)
    build_lhs(qnext_ref[0], i + 1, lhs_next_ref, qnorm_next_ref)
    softmax_group(0, sa_ref, own_block=True)

    knorm = jnp.sqrt(knorm_ref[0:2, 0:1])
    excess = (NORM_SLACK * qnorm_ref[...] * jnp.where(top_half[:, 0:1], knorm[0:1], knorm[1:2])
              - m_ref[...])
    a0 = jnp.max(excess[0:blk], axis=0, keepdims=True)
    a1 = jnp.max(excess[blk:2 * blk], axis=0, keepdims=True)
    edge = (blk - 1.0) / blk
    x0 = (UNDERFLOW_LOG2 - a0) * islopes_ref[2 * p] - edge
    x1 = (UNDERFLOW_LOG2 - a1) * islopes_ref[2 * p + 1] - edge
    i_f = jnp.full((1, 1), i, jnp.int32).astype(F32)
    first_needed = jnp.clip(jnp.floor(jnp.minimum(x0, x1)), 0.0, i_f)
    left = jnp.maximum(i_f - first_needed - (ATTN_GROUP - 1.0), 0.0)
    ngroups = jnp.ceil(left * (1.0 / ATTN_GROUP)).astype(jnp.int32)[0, 0]
    npair = ngroups // 2

    def body(u, carry):
        scores(2 * u + 2, sa_ref)
        softmax_group(2 * u + 1, sb_ref)
        scores(jnp.minimum(2 * u + 3, ngroups), sb_ref)
        softmax_group(2 * u + 2, sa_ref)
        return carry

    lax.fori_loop(0, npair, body, 0)

    @pl.when((ngroups & 1) == 1)
    def _():
        softmax_group(ngroups, sb_ref)

    a0 = acc_ref[0:blk, :]
    a1 = acc_ref[blk:2 * blk, :]
    l0 = a0[:, HEAD_DIM:HEAD_DIM + 1]
    l1 = a1[:, 0:1]
    o_ref[0] = jnp.where(head0, a0 / l0, a1 / l1).astype(o_ref.dtype)


def _attention(q, k, v):
    b, s, d = q.shape
    blk = MOBA_BLOCK
    nb = s // blk
    assert s % blk == 0 and nb <= HEAD_DIM, "block-select lanes hold at most 64 blocks"
    slopes = 2.0 ** (-8.0 * jnp.arange(1, N_HEADS + 1, dtype=F32) / N_HEADS)
    islopes = 1.0 / (slopes * (LOG2E * blk))
    pairs = d // LANES
    sp = s + PAD_BLOCKS * blk
    full = pl.BlockSpec((1, s, LANES), lambda bi, pi, ii: (bi, 0, pi))
    tile = pl.BlockSpec((1, blk, LANES), lambda bi, pi, ii: (bi, ii, pi))
    next_tile = pl.BlockSpec((1, blk, LANES), lambda bi, pi, ii: (bi, jnp.minimum(ii + 1, nb - 1), pi))
    smem = pl.BlockSpec(memory_space=pltpu.SMEM)
    return pl.pallas_call(
        functools.partial(_attn_kernel, nb=nb),
        grid=(b, pairs, nb),
        in_specs=[smem, smem, tile, next_tile, full, full],
        out_specs=tile,
        out_shape=jax.ShapeDtypeStruct((b, s, d), BF16),
        scratch_shapes=[pltpu.VMEM((LANES, LANES), F32),
                        pltpu.VMEM((8, LANES), F32),
                        pltpu.VMEM((sp, 2 * LANES), BF16),
                        pltpu.VMEM((sp, LANES), BF16),
                        pltpu.VMEM((sp, LANES), BF16),
                        pltpu.VMEM((2 * blk, 2 * LANES), BF16),
                        pltpu.VMEM((2 * blk, 2 * LANES), BF16),
                        pltpu.VMEM((2 * blk, 1), F32),
                        pltpu.VMEM((2 * blk, 1), F32),
                        pltpu.VMEM((2 * blk, ATTN_GROUP * blk), F32),
                        pltpu.VMEM((2 * blk, ATTN_GROUP * blk), F32),
                        pltpu.VMEM((2 * blk, 1), F32),
                        pltpu.VMEM((2 * blk, LANES), F32)],
        compiler_params=_cparams(3),
        name="moba_attn",
    )(slopes, islopes, q, q, k, v)


def _oproj_kernel(o_ref, x_ref, mod_ref, w_ref, out_ref):
    y = jnp.dot(o_ref[0], w_ref[...], preferred_element_type=F32)
    out_ref[0] = x_ref[0] + mod_ref[0][2:3] * y


def _oproj(o, x, mod, w_o):
    b, s, d = x.shape
    ts = 512
    blk = pl.BlockSpec((1, ts, d), lambda bi, si: (bi, si, 0))
    return pl.pallas_call(
        _oproj_kernel,
        grid=(b, s // ts),
        in_specs=[blk, blk,
                  pl.BlockSpec((1, 6, d), lambda bi, si: (bi, 0, 0)),
                  pl.BlockSpec((d, d), lambda bi, si: (0, 0))],
        out_specs=blk,
        out_shape=jax.ShapeDtypeStruct((b, s, d), F32),
        compiler_params=_cparams(2),
        name="attn_oproj",
    )(o, x, mod, w_o.astype(BF16))


def kernel(x, c, ada_w, ada_b, norm_mix_g, norm_ffn_g, pool_w, pool_scale, w_qkv, w_o,
           router_group_w, router_expert_w, exp_w_gate, exp_w_up, exp_w_down, final_norm_g):
    b, s, d = x.shape
    depth = ada_w.shape[0]
    mod = _adaln(c, ada_w, ada_b)
    for i in range(depth):
        if i % 2 == 0:
            x = _pool_layer(x, mod[i], norm_mix_g[i], pool_w[i // 2], pool_scale[i // 2])
        else:
            q, k, v = _qkv(x, mod[i], norm_mix_g[i], w_qkv[i // 2])
            o = _attention(q, k, v)
            x = _oproj(o, x, mod[i], w_o[i // 2])
        x = _moe_layer(x.reshape(b * s, d), mod[i], norm_ffn_g[i], router_group_w[i],
                       router_expert_w[i], exp_w_gate, exp_w_up, exp_w_down, i,
                       final_norm_g, s, final_norm=(i == depth - 1)).reshape(b, s, d)
    return x
```

```python
import functools

import jax
import jax.numpy as jnp
from jax import lax
from jax.experimental import pallas as pl
from jax.experimental.pallas import tpu as pltpu

F32 = jnp.float32
BF16 = jnp.bfloat16
HIGHEST = lax.Precision.HIGHEST

RMS_EPS = 1e-6
POOL_WINDOWS = (2, 4, 8, 16)
POOL_HALO = 16
N_HEADS = 16
HEAD_DIM = 64
MOBA_BLOCK = 256
MOBA_TOPK = 3
ATTN_GROUP = 4
N_EXPERT_GROUPS = 4
EXPERTS_PER_GROUP = 8
N_EXPERTS = N_EXPERT_GROUPS * EXPERTS_PER_GROUP
LANES = 128
NEG_BIG = -1e30
LOG2E = 1.4426950408889634
VMEM_LIMIT = 56 * 1024 * 1024


def _cparams(n_axes):
    return pltpu.CompilerParams(dimension_semantics=("arbitrary",) * n_axes,
                                vmem_limit_bytes=VMEM_LIMIT)


def _rms_mod(x, g, sc, sh):
    r = lax.rsqrt(jnp.mean(x * x, axis=-1, keepdims=True) + RMS_EPS)
    return (x * r) * g * (1.0 + sc) + sh


def _adaln_kernel(c_ref, w_ref, b_ref, o_ref):
    c = c_ref[...]
    ca = c / (1.0 + jnp.exp(-c))
    o_ref[0] = jnp.dot(ca, w_ref[0], precision=HIGHEST, preferred_element_type=F32) + b_ref[0]


def _adaln(c, ada_w, ada_b):
    depth, d, n = ada_w.shape
    b = c.shape[0]
    rows = 8
    cp = jnp.zeros((rows, d), F32).at[:b].set(c)
    tn = 1536
    out = pl.pallas_call(
        _adaln_kernel,
        grid=(depth, n // tn),
        in_specs=[pl.BlockSpec((rows, d), lambda i, j: (0, 0)),
                  pl.BlockSpec((1, d, tn), lambda i, j: (i, 0, j)),
                  pl.BlockSpec((1, 1, tn), lambda i, j: (i, 0, j))],
        out_specs=pl.BlockSpec((1, rows, tn), lambda i, j: (i, 0, j)),
        out_shape=jax.ShapeDtypeStruct((depth, rows, n), F32),
        compiler_params=_cparams(2),
        name="adaln",
    )(cp, ada_w, ada_b.reshape(depth, 1, n))
    return out[:, :b].reshape(depth, b, 6, d)


def _pool_kernel(x_ref, mod_ref, g_ref, w_ref, ps_ref, o_ref, hb_ref, *, ts):
    s = pl.program_id(1)

    @pl.when(s == 0)
    def _():
        hb_ref[0:POOL_HALO, :] = jnp.zeros((POOL_HALO, hb_ref.shape[1]), F32)

    x = x_ref[0]
    mod = mod_ref[0]
    h = _rms_mod(x, g_ref[...], mod[1:2], mod[0:1])
    hb_ref[POOL_HALO:POOL_HALO + ts, :] = h
    pos = s * ts + lax.broadcasted_iota(jnp.int32, (ts, 1), 0)
    gd = x.shape[1] // len(POOL_WINDOWS)
    ys = []
    for gi, w in enumerate(POOL_WINDOWS):
        c0 = gi * gd
        hg = h[:, c0:c0 + gd]
        acc = hg
        for k in range(1, w):
            acc = acc + hb_ref[POOL_HALO - k:POOL_HALO - k + ts, c0:c0 + gd]
        cnt = jnp.minimum(pos + 1, w).astype(F32)
        p = (acc / cnt - hg).astype(BF16)
        ys.append(jnp.dot(p, w_ref[gi], preferred_element_type=F32))
    y = jnp.concatenate(ys, axis=-1) * ps_ref[...]
    o_ref[0] = x + mod[2:3] * y
    hb_ref[0:POOL_HALO, :] = hb_ref[ts:ts + POOL_HALO, :]


def _pool_layer(x, mod, norm_g, pool_w, pool_scale):
    b, s, d = x.shape
    ts = 512
    g = len(POOL_WINDOWS)
    return pl.pallas_call(
        functools.partial(_pool_kernel, ts=ts),
        grid=(b, s // ts),
        in_specs=[pl.BlockSpec((1, ts, d), lambda bi, si: (bi, si, 0)),
                  pl.BlockSpec((1, 6, d), lambda bi, si: (bi, 0, 0)),
                  pl.BlockSpec((1, d), lambda bi, si: (0, 0)),
                  pl.BlockSpec((g, d // g, d // g), lambda bi, si: (0, 0, 0)),
                  pl.BlockSpec((1, d), lambda bi, si: (0, 0))],
        out_specs=pl.BlockSpec((1, ts, d), lambda bi, si: (bi, si, 0)),
        out_shape=jax.ShapeDtypeStruct((b, s, d), F32),
        scratch_shapes=[pltpu.VMEM((POOL_HALO + ts, d), F32)],
        compiler_params=_cparams(2),
        name="pool_mixer",
    )(x, mod, norm_g.reshape(1, d), pool_w.astype(BF16), pool_scale.reshape(1, d))


def _route(logits):
    lane = lax.broadcasted_iota(jnp.int32, logits.shape, 1).astype(F32)
    big = 1e9

    def first_argmax(v, vmax):
        return jnp.min(jnp.where(v == vmax, lane, big), axis=-1, keepdims=True)

    is_g = (lane >= N_EXPERTS) & (lane < N_EXPERTS + N_EXPERT_GROUPS)
    glog = jnp.where(is_g, logits, -jnp.inf)
    gmax = jnp.max(glog, axis=-1, keepdims=True)
    gsel = first_argmax(glog, gmax) - N_EXPERTS
    gsum = jnp.sum(jnp.exp(glog - gmax), axis=-1, keepdims=True)
    g_w = 1.0 / gsum
    lo = gsel * EXPERTS_PER_GROUP
    in_grp = (lane >= lo) & (lane < lo + EXPERTS_PER_GROUP)
    elog = jnp.where(in_grp, logits, -jnp.inf)
    m1 = jnp.max(elog, axis=-1, keepdims=True)
    i1 = first_argmax(elog, m1)
    elog2 = jnp.where(lane == i1, -jnp.inf, elog)
    m2 = jnp.max(elog2, axis=-1, keepdims=True)
    i2 = first_argmax(elog2, m2)
    d = jnp.exp(m2 - m1)
    w1 = g_w / (1.0 + d)
    w2 = g_w * d / (1.0 + d)
    return i1.astype(jnp.int32), i2.astype(jnp.int32), w1, w2


MOE_TILE = 512
ROUTE_TILE = 512
MOVE_TILE = 1024
_INFO_ROWS = 8
SUBLANES = 8
ROW_COPY_UNROLL = 16


def _store_row_tiles(ref, v):
    groups = v.shape[1] // LANES
    for s in range(groups):
        ref[pl.ds(s, v.shape[0], stride=groups), :] = v[:, s * LANES:(s + 1) * LANES]


def _load_row_tiles(ref, groups):
    rows = ref.shape[0] // groups
    return jnp.concatenate([ref[pl.ds(s, rows, stride=groups), :] for s in range(groups)], axis=1)


def _row_tile(ref, r, groups):
    return ref.at[pl.ds(pl.multiple_of(r * groups, groups), groups), :]


def _route_kernel(x_ref, mod_ref, g_ref, wr_ref, hp_ref, info_ref, wts_ref, cnt_ref, carry_ref):
    t = pl.program_id(0)

    @pl.when(t == 0)
    def _():
        carry_ref[...] = jnp.zeros_like(carry_ref)

    mod = mod_ref[0]
    h = _rms_mod(x_ref[...], g_ref[...], mod[4:5], mod[3:4])
    _store_row_tiles(hp_ref, h)
    wr = wr_ref[...]
    h_hi = h.astype(BF16)
    h_lo = (h - h_hi.astype(F32)).astype(BF16)
    w_hi = wr.astype(BF16)
    w_lo = (wr - w_hi.astype(F32)).astype(BF16)
    logits = (jnp.dot(h_hi, w_hi, preferred_element_type=F32)
              + jnp.dot(h_lo, w_hi, preferred_element_type=F32)
              + jnp.dot(h_hi, w_lo, preferred_element_type=F32))
    i1, i2, w1, w2 = _route(logits)
    tm = logits.shape[0]
    lane = lax.broadcasted_iota(jnp.int32, logits.shape, 1)
    m1 = lane == i1
    m2 = lane == i2
    mask = jnp.where(m1 | m2, 1.0, 0.0)
    earlier = (lax.broadcasted_iota(jnp.int32, (tm, tm), 0)
               > lax.broadcasted_iota(jnp.int32, (tm, tm), 1))
    rank = jnp.dot(jnp.where(earlier, 1.0, 0.0).astype(BF16), mask.astype(BF16),
                   preferred_element_type=F32) + carry_ref[...]
    rank1 = jnp.sum(jnp.where(m1, rank, 0.0), axis=-1, keepdims=True)
    rank2 = jnp.sum(jnp.where(m2, rank, 0.0), axis=-1, keepdims=True)
    carry_ref[...] += jnp.sum(mask, axis=0, keepdims=True)
    cnt_ref[...] = carry_ref[...]
    cols = (i1.astype(F32), i2.astype(F32), rank1, rank2)
    z = jnp.zeros(logits.shape, F32)
    for c, col in enumerate(cols):
        z = jnp.where(lane == c, col, z)
    info_ref[0] = z.T[0:_INFO_ROWS, :].astype(jnp.int32)
    wts_ref[...] = jnp.where(lane == 0, w1, jnp.where(lane == 1, w2, 0.0))


def _positions_kernel(base_ref, info_ref, pos_ref, *, n_experts):
    info = info_ref[0]
    expert = info[0:2]
    start = jnp.zeros_like(expert)
    for k in range(n_experts):
        start = jnp.where(expert == k, base_ref[k], start)
    pos_ref[0] = jnp.concatenate([start + info[2:4], info[2:_INFO_ROWS]], axis=0)


def _row_positions(info_ref, r):
    return info_ref[0, r], info_ref[1, r]


def _load_info(info_hbm, info_ref, sem):
    cp = pltpu.make_async_copy(info_hbm.at[pl.program_id(0)], info_ref, sem)
    cp.start()
    cp.wait()


def _scatter_kernel(ends_ref, info_hbm, hp_ref, hs_ref, info_ref, zero_ref, isem, sem,
                    *, n_experts, n_tiles):
    @pl.when(pl.program_id(0) == 0)
    def _():
        zero_ref[...] = jnp.zeros_like(zero_ref)

        def zero_tile(row0):
            dst = hs_ref.at[pl.ds(pl.multiple_of(row0 * SUBLANES, SUBLANES), MOE_TILE * SUBLANES), :]
            return pltpu.make_async_copy(zero_ref, dst, sem)

        def issue_e(e, issued):
            end = ends_ref[e]
            start = jnp.where(e == 0, 0, ends_ref[jnp.maximum(e - 1, 0)])
            nonempty = end > start

            @pl.when(nonempty)
            def _():
                zero_tile(pl.multiple_of(end - MOE_TILE, MOE_TILE)).start()

            return issued + nonempty.astype(jnp.int32)

        def issue_tail(s, carry):
            zero_tile(pl.multiple_of(s * MOE_TILE, MOE_TILE)).start()
            return carry

        first_tail = ends_ref[n_experts - 1] // MOE_TILE
        issued = lax.fori_loop(0, n_experts, issue_e, 0)
        lax.fori_loop(first_tail, n_tiles, issue_tail, 0)

        def drain_zero(s, carry):
            zero_tile(0).wait()
            return carry

        lax.fori_loop(0, issued + n_tiles - first_tail, drain_zero, 0)

    _load_info(info_hbm, info_ref, isem)

    def row_copy(r, pos):
        return pltpu.make_async_copy(_row_tile(hp_ref, r, SUBLANES), _row_tile(hs_ref, pos, SUBLANES),
                                     sem)

    def issue(r, carry):
        for pos in _row_positions(info_ref, r):
            row_copy(r, pos).start()
        return carry

    def drain(r, carry):
        row_copy(0, 0).wait()
        row_copy(0, 0).wait()
        return carry

    lax.fori_loop(0, MOVE_TILE, issue, 0, unroll=ROW_COPY_UNROLL)
    lax.fori_loop(0, MOVE_TILE, drain, 0, unroll=2 * ROW_COPY_UNROLL)


def _expert_kernel(te_ref, hs_ref, wg_ref, wu_ref, wd_ref, ys_ref):
    del te_ref
    h = _load_row_tiles(hs_ref, SUBLANES).astype(BF16)
    gt = jnp.dot(h, wg_ref[0, 0].astype(BF16), preferred_element_type=F32)
    up = jnp.dot(h, wu_ref[0, 0].astype(BF16), preferred_element_type=F32)
    act = (gt / (1.0 + jnp.exp(-gt))) * up
    y = jnp.dot(act.astype(BF16), wd_ref[0, 0].astype(BF16), preferred_element_type=F32)
    _store_row_tiles(ys_ref, y)


def _combine_kernel(info_hbm, x_ref, wts_ref, mod_ref, fg_ref, ys_ref, o_ref,
                    info_ref, y1_ref, y2_ref, isem, sem, *, final_norm):
    _load_info(info_hbm, info_ref, isem)

    def row_copy(pos, dst_ref, r):
        return pltpu.make_async_copy(_row_tile(ys_ref, pos, SUBLANES), _row_tile(dst_ref, r, SUBLANES),
                                     sem)

    def issue(r, carry):
        p1, p2 = _row_positions(info_ref, r)
        row_copy(p1, y1_ref, r).start()
        row_copy(p2, y2_ref, r).start()
        return carry

    def drain(r, carry):
        row_copy(0, y1_ref, 0).wait()
        row_copy(0, y2_ref, 0).wait()
        return carry

    lax.fori_loop(0, MOVE_TILE, issue, 0, unroll=ROW_COPY_UNROLL)
    lax.fori_loop(0, MOVE_TILE, drain, 0, unroll=2 * ROW_COPY_UNROLL)
    wts = wts_ref[...]
    y = (wts[:, 0:1] * _load_row_tiles(y1_ref, SUBLANES)
         + wts[:, 1:2] * _load_row_tiles(y2_ref, SUBLANES))
    xn = x_ref[...] + mod_ref[0][5:6] * y
    if final_norm:
        r = lax.rsqrt(jnp.mean(xn * xn, axis=-1, keepdims=True) + RMS_EPS)
        xn = (xn * r) * fg_ref[...]
    o_ref[...] = xn


def _moe_layer(x2, mod, norm_g, w_group, w_expert, w_gate, w_up, w_down, layer, final_g, seq,
               final_norm):
    t, d = x2.shape
    _, e, _, f = w_gate.shape
    assert d == SUBLANES * LANES, "a token row must fill exactly one (8, 128) tile"
    n_move = t // MOVE_TILE
    route_per_move = MOVE_TILE // ROUTE_TILE
    n_tiles = (2 * t) // MOE_TILE + e
    wr = jnp.zeros((d, LANES), F32).at[:, :e].set(w_expert).at[:, e:e + N_EXPERT_GROUPS].set(w_group)
    anyspace = pl.BlockSpec(memory_space=pl.ANY)

    hp, info, wts, cnt = pl.pallas_call(
        _route_kernel,
        grid=(t // ROUTE_TILE,),
        in_specs=[pl.BlockSpec((ROUTE_TILE, d), lambda ti: (ti, 0)),
                  pl.BlockSpec((1, 6, d), lambda ti: (ti * ROUTE_TILE // seq, 0, 0)),
                  pl.BlockSpec((1, d), lambda ti: (0, 0)),
                  pl.BlockSpec((d, LANES), lambda ti: (0, 0))],
        out_specs=[pl.BlockSpec((ROUTE_TILE * SUBLANES, LANES), lambda ti: (ti, 0)),
                   pl.BlockSpec((1, _INFO_ROWS, ROUTE_TILE),
                                lambda ti: (ti // route_per_move, 0, ti % route_per_move)),
                   pl.BlockSpec((ROUTE_TILE, LANES), lambda ti: (ti, 0)),
                   pl.BlockSpec((1, LANES), lambda ti: (0, 0))],
        out_shape=[jax.ShapeDtypeStruct((t * SUBLANES, LANES), F32),
                   jax.ShapeDtypeStruct((n_move, _INFO_ROWS, MOVE_TILE), jnp.int32),
                   jax.ShapeDtypeStruct((t, LANES), F32),
                   jax.ShapeDtypeStruct((1, LANES), F32)],
        scratch_shapes=[pltpu.VMEM((1, LANES), F32)],
        compiler_params=_cparams(1),
        name="moe_route",
    )(x2, mod, norm_g.reshape(1, d), wr)

    counts = cnt[0, :e].astype(jnp.int32)
    padded = (counts + MOE_TILE - 1) // MOE_TILE * MOE_TILE
    ends = jnp.cumsum(padded)
    base = ends - padded
    tile_start = jnp.arange(n_tiles, dtype=jnp.int32) * MOE_TILE
    tile_expert = jnp.minimum(jnp.sum((ends[None, :] <= tile_start[:, None]).astype(jnp.int32), axis=1),
                              e - 1)

    info_blk = pl.BlockSpec((1, _INFO_ROWS, MOVE_TILE), lambda ti, b: (ti, 0, 0))
    pos = pl.pallas_call(
        functools.partial(_positions_kernel, n_experts=e),
        grid_spec=pltpu.PrefetchScalarGridSpec(
            num_scalar_prefetch=1, grid=(n_move,), in_specs=[info_blk], out_specs=info_blk),
        out_shape=jax.ShapeDtypeStruct(info.shape, jnp.int32),
        compiler_params=_cparams(1),
        name="moe_positions",
    )(base, info)

    hs = pl.pallas_call(
        functools.partial(_scatter_kernel, n_experts=e, n_tiles=n_tiles),
        grid_spec=pltpu.PrefetchScalarGridSpec(
            num_scalar_prefetch=1,
            grid=(n_move,),
            in_specs=[anyspace,
                      pl.BlockSpec((MOVE_TILE * SUBLANES, LANES), lambda ti, en: (ti, 0))],
            out_specs=anyspace,
            scratch_shapes=[pltpu.SMEM((_INFO_ROWS, MOVE_TILE), jnp.int32),
                            pltpu.VMEM((MOE_TILE * SUBLANES, LANES), F32),
                            pltpu.SemaphoreType.DMA, pltpu.SemaphoreType.DMA]),
        out_shape=jax.ShapeDtypeStruct((n_tiles * MOE_TILE * SUBLANES, LANES), F32),
        compiler_params=_cparams(1),
        name="moe_scatter",
    )(ends, pos, hp)

    ys = pl.pallas_call(
        _expert_kernel,
        grid_spec=pltpu.PrefetchScalarGridSpec(
            num_scalar_prefetch=1,
            grid=(n_tiles,),
            in_specs=[pl.BlockSpec((MOE_TILE * SUBLANES, LANES), lambda si, te: (si, 0)),
                      pl.BlockSpec((1, 1, d, f), lambda si, te: (layer, te[si], 0, 0)),
                      pl.BlockSpec((1, 1, d, f), lambda si, te: (layer, te[si], 0, 0)),
                      pl.BlockSpec((1, 1, f, d), lambda si, te: (layer, te[si], 0, 0))],
            out_specs=pl.BlockSpec((MOE_TILE * SUBLANES, LANES), lambda si, te: (si, 0))),
        out_shape=jax.ShapeDtypeStruct((n_tiles * MOE_TILE * SUBLANES, LANES), F32),
        compiler_params=_cparams(1),
        name="moe_experts",
    )(tile_expert, hs, w_gate, w_up, w_down)

    return pl.pallas_call(
        functools.partial(_combine_kernel, final_norm=final_norm),
        grid=(n_move,),
        in_specs=[anyspace,
                  pl.BlockSpec((MOVE_TILE, d), lambda ti: (ti, 0)),
                  pl.BlockSpec((MOVE_TILE, LANES), lambda ti: (ti, 0)),
                  pl.BlockSpec((1, 6, d), lambda ti: (ti * MOVE_TILE // seq, 0, 0)),
                  pl.BlockSpec((1, d), lambda ti: (0, 0)),
                  anyspace],
        out_specs=pl.BlockSpec((MOVE_TILE, d), lambda ti: (ti, 0)),
        scratch_shapes=[pltpu.SMEM((_INFO_ROWS, MOVE_TILE), jnp.int32),
                        pltpu.VMEM((MOVE_TILE * SUBLANES, LANES), F32),
                        pltpu.VMEM((MOVE_TILE * SUBLANES, LANES), F32),
                        pltpu.SemaphoreType.DMA, pltpu.SemaphoreType.DMA],
        out_shape=jax.ShapeDtypeStruct((t, d), F32),
        compiler_params=_cparams(1),
        name="moe_combine_final" if final_norm else "moe_combine",
    )(pos, x2, wts, mod, final_g.reshape(1, d), ys)


def _qkv_kernel(x_ref, mod_ref, g_ref, w_ref, q_ref, k_ref, v_ref):
    mod = mod_ref[0]
    h = _rms_mod(x_ref[0], g_ref[...], mod[1:2], mod[0:1]).astype(BF16)
    d = h.shape[1]
    q_ref[0] = jnp.dot(h, w_ref[:, 0:d], preferred_element_type=F32)
    k_ref[0] = jnp.dot(h, w_ref[:, d:2 * d], preferred_element_type=F32).astype(BF16)
    v_ref[0] = jnp.dot(h, w_ref[:, 2 * d:3 * d], preferred_element_type=F32).astype(BF16)


def _qkv(x, mod, norm_g, w_qkv):
    b, s, d = x.shape
    ts = 512
    blk = pl.BlockSpec((1, ts, d), lambda bi, si: (bi, si, 0))
    return pl.pallas_call(
        _qkv_kernel,
        grid=(b, s // ts),
        in_specs=[blk,
                  pl.BlockSpec((1, 6, d), lambda bi, si: (bi, 0, 0)),
                  pl.BlockSpec((1, d), lambda bi, si: (0, 0)),
                  pl.BlockSpec((d, 3 * d), lambda bi, si: (0, 0))],
        out_specs=[blk, blk, blk],
        out_shape=[jax.ShapeDtypeStruct((b, s, d), F32),
                   jax.ShapeDtypeStruct((b, s, d), BF16),
                   jax.ShapeDtypeStruct((b, s, d), BF16)],
        compiler_params=_cparams(2),
        name="qkv_proj",
    )(x, mod, norm_g.reshape(1, d), w_qkv.astype(BF16))


_SLOPE_LANE = 64
_BLKIDX_LANE = 67
_PAD_LANE = 70
PAD_BLOCKS = 2 * ATTN_GROUP
UNDERFLOW_LOG2 = -140.0
NORM_SLACK = 1.02


def _attn_kernel(slopes_ref, islopes_ref, q_ref, qnext_ref, k_ref, v_ref, o_ref,
                 kmean_ref, knorm_ref, kext_ref, v0_ref, v1_ref, lhs_ref, lhs_next_ref,
                 qnorm_ref, qnorm_next_ref, sa_ref, sb_ref, m_ref, acc_ref, *, nb):
    p = pl.program_id(1)
    i = pl.program_id(2)
    blk = MOBA_BLOCK
    pad = PAD_BLOCKS * blk
    lane = lax.broadcasted_iota(jnp.int32, (blk, LANES), 1)
    head0 = lane < HEAD_DIM

    nt = (((1,), (1,)), ((), ()))
    gw = ATTN_GROUP * blk
    lane2 = lax.broadcasted_iota(jnp.int32, (2 * blk, LANES), 1)
    top_half = lax.broadcasted_iota(jnp.int32, (2 * blk, LANES), 0) < blk

    def build_lhs(qblk, own, lhs_dst, qnorm_dst):
        q = qblk * (HEAD_DIM ** -0.5 * LOG2E)
        qs = jnp.concatenate([jnp.where(head0, q, 0.0), jnp.where(head0, 0.0, q)], axis=0)
        gate = lax.dot_general(kmean_ref[0:HEAD_DIM, :], qs, nt, precision=HIGHEST,
                               preferred_element_type=F32)
        blk_i = lax.broadcasted_iota(jnp.int32, gate.shape, 0)
        blk_f = blk_i.astype(F32)
        valid = blk_i < own
        gate = jnp.where(valid, gate, -jnp.inf)
        sel = blk_i == own
        for _ in range(MOBA_TOPK):
            mx = jnp.max(gate, axis=0, keepdims=True)
            idx = jnp.min(jnp.where(gate == mx, blk_f, 1e9), axis=0, keepdims=True)
            hit = blk_f == idx
            sel = sel | (hit & valid)
            gate = jnp.where(hit, -jnp.inf, gate)
        bias = jnp.where(sel, 0.0, NEG_BIG)
        ext = jnp.concatenate([bias, jnp.full(bias.shape, NEG_BIG, F32)], axis=0).T
        slope = jnp.where(top_half, slopes_ref[2 * p], slopes_ref[2 * p + 1]) * LOG2E
        s_hi = slope.astype(BF16).astype(F32)
        rem = slope - s_hi
        s_mid = rem.astype(BF16).astype(F32)
        s_lo = rem - s_mid
        for k, part in enumerate((s_hi, s_mid, s_lo)):
            ext = jnp.where(lane2 == _SLOPE_LANE + k, part, ext)
            ext = jnp.where(lane2 == _BLKIDX_LANE + k, part * float(blk), ext)
        ext = jnp.where(lane2 > _PAD_LANE, 0.0, ext)
        lhs_dst[:, 0:LANES] = qs.astype(BF16)
        lhs_dst[:, LANES:2 * LANES] = ext.astype(BF16)
        qnorm_dst[...] = jnp.sqrt(jnp.sum(qs * qs, axis=-1, keepdims=True))

    @pl.when(i == 0)
    def _():
        kmean_ref[...] = jnp.zeros_like(kmean_ref)
        knorm_ref[...] = jnp.zeros_like(knorm_ref)
        kext_ref[0:pad, 0:LANES] = jnp.zeros((pad, LANES), BF16)
        lane_p = lax.broadcasted_iota(jnp.int32, (pad, LANES), 1)
        kext_ref[0:pad, LANES:2 * LANES] = jnp.where(lane_p == _PAD_LANE, 1.0, 0.0).astype(BF16)
        v0_ref[0:pad, :] = jnp.zeros((pad, LANES), BF16)
        v1_ref[0:pad, :] = jnp.zeros((pad, LANES), BF16)
        row = lax.broadcasted_iota(jnp.int32, (blk, LANES), 0).astype(F32)
        in_slope = (lane >= _SLOPE_LANE) & (lane < _SLOPE_LANE + 3)
        in_blk = (lane >= _BLKIDX_LANE) & (lane < _BLKIDX_LANE + 3)
        ones0 = jnp.where(lane == HEAD_DIM, 1.0, 0.0)
        ones1 = jnp.where(lane == 0, 1.0, 0.0)

        def build(j, carry):
            src = pl.ds(pl.multiple_of(j * blk, blk), blk)
            dst = pl.ds(pl.multiple_of(j * blk + pad, blk), blk)
            kj = k_ref[0, src, :]
            kf = kj.astype(F32)
            kmean_ref[pl.ds(j, 1), :] = jnp.mean(kf, axis=0, keepdims=True)
            ksq = kf * kf
            n0 = jnp.max(jnp.sum(jnp.where(head0, ksq, 0.0), axis=-1, keepdims=True),
                         axis=0, keepdims=True)
            n1 = jnp.max(jnp.sum(jnp.where(head0, 0.0, ksq), axis=-1, keepdims=True),
                         axis=0, keepdims=True)
            knorm_ref[0:1, :] = jnp.maximum(knorm_ref[0:1, :], n0)
            knorm_ref[1:2, :] = jnp.maximum(knorm_ref[1:2, :], n1)
            jf = jnp.full((blk, LANES), j, jnp.int32).astype(F32)
            extra = jnp.where(lane == j, 1.0, jnp.where(in_slope, row, jnp.where(in_blk, jf, 0.0)))
            kext_ref[dst, 0:LANES] = kj
            kext_ref[dst, LANES:2 * LANES] = extra.astype(BF16)
            vj = v_ref[0, src, :].astype(F32)
            v0_ref[dst, :] = jnp.where(head0, vj, ones0).astype(BF16)
            v1_ref[dst, :] = jnp.where(head0, ones1, vj).astype(BF16)
            return carry

        lax.fori_loop(0, nb, build, 0, unroll=2)

        build_lhs(q_ref[0], i, lhs_next_ref, qnorm_next_ref)

    lhs_ref[...] = lhs_next_ref[...]
    qnorm_ref[...] = qnorm_next_ref[...]
    m_ref[...] = jnp.full(m_ref.shape, NEG_BIG, F32)
    acc_ref[...] = jnp.zeros_like(acc_ref)

    def group_rows(t):
        start = (i - ATTN_GROUP * (t + 1) + 1 + PAD_BLOCKS) * blk
        return pl.ds(pl.multiple_of(start, blk), gw)

    def scores(t, dst_ref):
        dst_ref[...] = lax.dot_general(lhs_ref[...], kext_ref[group_rows(t), :], nt,
                                       preferred_element_type=F32)

    def softmax_group(t, src_ref, own_block=False):
        rows = group_rows(t)
        for hh in range(2):
            r = slice(hh * blk, (hh + 1) * blk)
            s = src_ref[r, :]
            if own_block:
                qi = lax.broadcasted_iota(jnp.int32, (blk, blk), 0)
                ki = lax.broadcasted_iota(jnp.int32, (blk, blk), 1)
                s = jnp.concatenate([s[:, :gw - blk],
                                     jnp.where(ki <= qi, s[:, gw - blk:], NEG_BIG)], axis=1)
            m_old = m_ref[r, :]
            m_new = jnp.maximum(m_old, jnp.max(s, axis=-1, keepdims=True))
            alpha = jnp.exp2(m_old - m_new)
            pm = jnp.exp2(s - m_new).astype(BF16)
            vref = v0_ref if hh == 0 else v1_ref
            pv = jnp.dot(pm, vref[rows, :], preferred_element_type=F32)
            acc_ref[r, :] = alpha * acc_ref[r, :] + pv
            m_ref[r, :] = m_new

    scores(0, sa_ref)
    scores(1, sb_ref)
    build_lhs(qnext_ref[0], i + 1, lhs_next_ref, qnorm_next_ref)
    softmax_group(0, sa_ref, own_block=True)

    knorm = jnp.sqrt(knorm_ref[0:2, 0:1])
    excess = (NORM_SLACK * qnorm_ref[...] * jnp.where(top_half[:, 0:1], knorm[0:1], knorm[1:2])
              - m_ref[...])
    a0 = jnp.max(excess[0:blk], axis=0, keepdims=True)
    a1 = jnp.max(excess[blk:2 * blk], axis=0, keepdims=True)
    edge = (blk - 1.0) / blk
    x0 = (UNDERFLOW_LOG2 - a0) * islopes_ref[2 * p] - edge
    x1 = (UNDERFLOW_LOG2 - a1) * islopes_ref[2 * p + 1] - edge
    i_f = jnp.full((1, 1), i, jnp.int32).astype(F32)
    first_needed = jnp.clip(jnp.floor(jnp.minimum(x0, x1)), 0.0, i_f)
    left = jnp.maximum(i_f - first_needed - (ATTN_GROUP - 1.0), 0.0)
    ngroups = jnp.ceil(left * (1.0 / ATTN_GROUP)).astype(jnp.int32)[0, 0]
    npair = ngroups // 2

    def body(u, carry):
        scores(2 * u + 2, sa_ref)
        softmax_group(2 * u + 1, sb_ref)
        scores(jnp.minimum(2 * u + 3, ngroups), sb_ref)
        softmax_group(2 * u + 2, sa_ref)
        return carry

    lax.fori_loop(0, npair, body, 0)

    @pl.when((ngroups & 1) == 1)
    def _():
        softmax_group(ngroups, sb_ref)

    a0 = acc_ref[0:blk, :]
    a1 = acc_ref[blk:2 * blk, :]
    l0 = a0[:, HEAD_DIM:HEAD_DIM + 1]
    l1 = a1[:, 0:1]
    o_ref[0] = jnp.where(head0, a0 / l0, a1 / l1).astype(o_ref.dtype)


def _attention(q, k, v):
    b, s, d = q.shape
    blk = MOBA_BLOCK
    nb = s // blk
    assert s % blk == 0 and nb <= HEAD_DIM, "block-select lanes hold at most 64 blocks"
    slopes = 2.0 ** (-8.0 * jnp.arange(1, N_HEADS + 1, dtype=F32) / N_HEADS)
    islopes = 1.0 / (slopes * (LOG2E * blk))
    pairs = d // LANES
    sp = s + PAD_BLOCKS * blk
    full = pl.BlockSpec((1, s, LANES), lambda bi, pi, ii: (bi, 0, pi))
    tile = pl.BlockSpec((1, blk, LANES), lambda bi, pi, ii: (bi, ii, pi))
    next_tile = pl.BlockSpec((1, blk, LANES), lambda bi, pi, ii: (bi, jnp.minimum(ii + 1, nb - 1), pi))
    smem = pl.BlockSpec(memory_space=pltpu.SMEM)
    return pl.pallas_call(
        functools.partial(_attn_kernel, nb=nb),
        grid=(b, pairs, nb),
        in_specs=[smem, smem, tile, next_tile, full, full],
        out_specs=tile,
        out_shape=jax.ShapeDtypeStruct((b, s, d), BF16),
        scratch_shapes=[pltpu.VMEM((LANES, LANES), F32),
                        pltpu.VMEM((8, LANES), F32),
                        pltpu.VMEM((sp, 2 * LANES), BF16),
                        pltpu.VMEM((sp, LANES), BF16),
                        pltpu.VMEM((sp, LANES), BF16),
                        pltpu.VMEM((2 * blk, 2 * LANES), BF16),
                        pltpu.VMEM((2 * blk, 2 * LANES), BF16),
                        pltpu.VMEM((2 * blk, 1), F32),
                        pltpu.VMEM((2 * blk, 1), F32),
                        pltpu.VMEM((2 * blk, ATTN_GROUP * blk), F32),
                        pltpu.VMEM((2 * blk, ATTN_GROUP * blk), F32),
                        pltpu.VMEM((2 * blk, 1), F32),
                        pltpu.VMEM((2 * blk, LANES), F32)],
        compiler_params=_cparams(3),
        name="moba_attn",
    )(slopes, islopes, q, q, k, v)


def _oproj_kernel(o_ref, x_ref, mod_ref, w_ref, out_ref):
    y = jnp.dot(o_ref[0], w_ref[...], preferred_element_type=F32)
    out_ref[0] = x_ref[0] + mod_ref[0][2:3] * y


def _oproj(o, x, mod, w_o):
    b, s, d = x.shape
    ts = 512
    blk = pl.BlockSpec((1, ts, d), lambda bi, si: (bi, si, 0))
    return pl.pallas_call(
        _oproj_kernel,
        grid=(b, s // ts),
        in_specs=[blk, blk,
                  pl.BlockSpec((1, 6, d), lambda bi, si: (bi, 0, 0)),
                  pl.BlockSpec((d, d), lambda bi, si: (0, 0))],
        out_specs=blk,
        out_shape=jax.ShapeDtypeStruct((b, s, d), F32),
        compiler_params=_cparams(2),
        name="attn_oproj",
    )(o, x, mod, w_o.astype(BF16))


def kernel(x, c, ada_w, ada_b, norm_mix_g, norm_ffn_g, pool_w, pool_scale, w_qkv, w_o,
           router_group_w, router_expert_w, exp_w_gate, exp_w_up, exp_w_down, final_norm_g):
    b, s, d = x.shape
    depth = ada_w.shape[0]
    mod = _adaln(c, ada_w, ada_b)
    for i in range(depth):
        if i % 2 == 0:
            x = _pool_layer(x, mod[i], norm_mix_g[i], pool_w[i // 2], pool_scale[i // 2])
        else:
            q, k, v = _qkv(x, mod[i], norm_mix_g[i], w_qkv[i // 2])
            o = _attention(q, k, v)
            x = _oproj(o, x, mod[i], w_o[i // 2])
        x = _moe_layer(x.reshape(b * s, d), mod[i], norm_ffn_g[i], router_group_w[i],
                       router_expert_w[i], exp_w_gate, exp_w_up, exp_w_down, i,
                       final_norm_g, s, final_norm=(i == depth - 1)).reshape(b, s, d)
    return x
```

```python
import functools

import jax
import jax.numpy as jnp
from jax import lax
from jax.experimental import pallas as pl
from jax.experimental.pallas import tpu as pltpu

F32 = jnp.float32
BF16 = jnp.bfloat16
HIGHEST = lax.Precision.HIGHEST

RMS_EPS = 1e-6
POOL_WINDOWS = (2, 4, 8, 16)
POOL_HALO = 16
N_HEADS = 16
HEAD_DIM = 64
MOBA_BLOCK = 256
MOBA_TOPK = 3
ATTN_GROUP = 4
N_EXPERT_GROUPS = 4
EXPERTS_PER_GROUP = 8
N_EXPERTS = N_EXPERT_GROUPS * EXPERTS_PER_GROUP
LANES = 128
NEG_BIG = -1e30
LOG2E = 1.4426950408889634
VMEM_LIMIT = 56 * 1024 * 1024


def _cparams(n_axes):
    return pltpu.CompilerParams(dimension_semantics=("arbitrary",) * n_axes,
                                vmem_limit_bytes=VMEM_LIMIT)


def _rms_mod(x, g, sc, sh):
    r = lax.rsqrt(jnp.mean(x * x, axis=-1, keepdims=True) + RMS_EPS)
    return (x * r) * g * (1.0 + sc) + sh


def _adaln_kernel(c_ref, w_ref, b_ref, o_ref):
    c = c_ref[...]
    ca = c / (1.0 + jnp.exp(-c))
    o_ref[0] = jnp.dot(ca, w_ref[0], precision=HIGHEST, preferred_element_type=F32) + b_ref[0]


def _adaln(c, ada_w, ada_b):
    depth, d, n = ada_w.shape
    b = c.shape[0]
    rows = 8
    cp = jnp.zeros((rows, d), F32).at[:b].set(c)
    tn = 1536
    out = pl.pallas_call(
        _adaln_kernel,
        grid=(depth, n // tn),
        in_specs=[pl.BlockSpec((rows, d), lambda i, j: (0, 0)),
                  pl.BlockSpec((1, d, tn), lambda i, j: (i, 0, j)),
                  pl.BlockSpec((1, 1, tn), lambda i, j: (i, 0, j))],
        out_specs=pl.BlockSpec((1, rows, tn), lambda i, j: (i, 0, j)),
        out_shape=jax.ShapeDtypeStruct((depth, rows, n), F32),
        compiler_params=_cparams(2),
        name="adaln",
    )(cp, ada_w, ada_b.reshape(depth, 1, n))
    return out[:, :b].reshape(depth, b, 6, d)


def _pool_kernel(x_ref, mod_ref, g_ref, w_ref, ps_ref, o_ref, hb_ref, *, ts):
    s = pl.program_id(1)

    @pl.when(s == 0)
    def _():
        hb_ref[0:POOL_HALO, :] = jnp.zeros((POOL_HALO, hb_ref.shape[1]), F32)

    x = x_ref[0]
    mod = mod_ref[0]
    h = _rms_mod(x, g_ref[...], mod[1:2], mod[0:1])
    hb_ref[POOL_HALO:POOL_HALO + ts, :] = h
    pos = s * ts + lax.broadcasted_iota(jnp.int32, (ts, 1), 0)
    gd = x.shape[1] // len(POOL_WINDOWS)
    ys = []
    for gi, w in enumerate(POOL_WINDOWS):
        c0 = gi * gd
        hg = h[:, c0:c0 + gd]
        acc = hg
        for k in range(1, w):
            acc = acc + hb_ref[POOL_HALO - k:POOL_HALO - k + ts, c0:c0 + gd]
        cnt = jnp.minimum(pos + 1, w).astype(F32)
        p = (acc / cnt - hg).astype(BF16)
        ys.append(jnp.dot(p, w_ref[gi], preferred_element_type=F32))
    y = jnp.concatenate(ys, axis=-1) * ps_ref[...]
    o_ref[0] = x + mod[2:3] * y
    hb_ref[0:POOL_HALO, :] = hb_ref[ts:ts + POOL_HALO, :]


def _pool_layer(x, mod, norm_g, pool_w, pool_scale):
    b, s, d = x.shape
    ts = 512
    g = len(POOL_WINDOWS)
    return pl.pallas_call(
        functools.partial(_pool_kernel, ts=ts),
        grid=(b, s // ts),
        in_specs=[pl.BlockSpec((1, ts, d), lambda bi, si: (bi, si, 0)),
                  pl.BlockSpec((1, 6, d), lambda bi, si: (bi, 0, 0)),
                  pl.BlockSpec((1, d), lambda bi, si: (0, 0)),
                  pl.BlockSpec((g, d // g, d // g), lambda bi, si: (0, 0, 0)),
                  pl.BlockSpec((1, d), lambda bi, si: (0, 0))],
        out_specs=pl.BlockSpec((1, ts, d), lambda bi, si: (bi, si, 0)),
        out_shape=jax.ShapeDtypeStruct((b, s, d), F32),
        scratch_shapes=[pltpu.VMEM((POOL_HALO + ts, d), F32)],
        compiler_params=_cparams(2),
        name="pool_mixer",
    )(x, mod, norm_g.reshape(1, d), pool_w.astype(BF16), pool_scale.reshape(1, d))


def _route(logits):
    lane = lax.broadcasted_iota(jnp.int32, logits.shape, 1).astype(F32)
    big = 1e9

    def first_argmax(v, vmax):
        return jnp.min(jnp.where(v == vmax, lane, big), axis=-1, keepdims=True)

    is_g = (lane >= N_EXPERTS) & (lane < N_EXPERTS + N_EXPERT_GROUPS)
    glog = jnp.where(is_g, logits, -jnp.inf)
    gmax = jnp.max(glog, axis=-1, keepdims=True)
    gsel = first_argmax(glog, gmax) - N_EXPERTS
    gsum = jnp.sum(jnp.exp(glog - gmax), axis=-1, keepdims=True)
    g_w = 1.0 / gsum
    lo = gsel * EXPERTS_PER_GROUP
    in_grp = (lane >= lo) & (lane < lo + EXPERTS_PER_GROUP)
    elog = jnp.where(in_grp, logits, -jnp.inf)
    m1 = jnp.max(elog, axis=-1, keepdims=True)
    i1 = first_argmax(elog, m1)
    elog2 = jnp.where(lane == i1, -jnp.inf, elog)
    m2 = jnp.max(elog2, axis=-1, keepdims=True)
    i2 = first_argmax(elog2, m2)
    d = jnp.exp(m2 - m1)
    w1 = g_w / (1.0 + d)
    w2 = g_w * d / (1.0 + d)
    return i1.astype(jnp.int32), i2.astype(jnp.int32), w1, w2


MOE_TILE = 512
ROUTE_TILE = 512
MOVE_TILE = 1024
_INFO_ROWS = 8
SUBLANES = 8
ROW_COPY_UNROLL = 16


def _store_row_tiles(ref, v):
    groups = v.shape[1] // LANES
    for s in range(groups):
        ref[pl.ds(s, v.shape[0], stride=groups), :] = v[:, s * LANES:(s + 1) * LANES]


def _load_row_tiles(ref, groups):
    rows = ref.shape[0] // groups
    return jnp.concatenate([ref[pl.ds(s, rows, stride=groups), :] for s in range(groups)], axis=1)


def _row_tile(ref, r, groups):
    return ref.at[pl.ds(pl.multiple_of(r * groups, groups), groups), :]


def _route_kernel(x_ref, mod_ref, g_ref, wr_ref, hp_ref, info_ref, wts_ref, cnt_ref, carry_ref):
    t = pl.program_id(0)

    @pl.when(t == 0)
    def _():
        carry_ref[...] = jnp.zeros_like(carry_ref)

    mod = mod_ref[0]
    h = _rms_mod(x_ref[...], g_ref[...], mod[4:5], mod[3:4])
    _store_row_tiles(hp_ref, h)
    wr = wr_ref[...]
    h_hi = h.astype(BF16)
    h_lo = (h - h_hi.astype(F32)).astype(BF16)
    w_hi = wr.astype(BF16)
    w_lo = (wr - w_hi.astype(F32)).astype(BF16)
    logits = (jnp.dot(h_hi, w_hi, preferred_element_type=F32)
              + jnp.dot(h_lo, w_hi, preferred_element_type=F32)
              + jnp.dot(h_hi, w_lo, preferred_element_type=F32))
    i1, i2, w1, w2 = _route(logits)
    tm = logits.shape[0]
    lane = lax.broadcasted_iota(jnp.int32, logits.shape, 1)
    m1 = lane == i1
    m2 = lane == i2
    mask = jnp.where(m1 | m2, 1.0, 0.0)
    earlier = (lax.broadcasted_iota(jnp.int32, (tm, tm), 0)
               > lax.broadcasted_iota(jnp.int32, (tm, tm), 1))
    rank = jnp.dot(jnp.where(earlier, 1.0, 0.0).astype(BF16), mask.astype(BF16),
                   preferred_element_type=F32) + carry_ref[...]
    rank1 = jnp.sum(jnp.where(m1, rank, 0.0), axis=-1, keepdims=True)
    rank2 = jnp.sum(jnp.where(m2, rank, 0.0), axis=-1, keepdims=True)
    carry_ref[...] += jnp.sum(mask, axis=0, keepdims=True)
    cnt_ref[...] = carry_ref[...]
    cols = (i1.astype(F32), i2.astype(F32), rank1, rank2)
    z = jnp.zeros(logits.shape, F32)
    for c, col in enumerate(cols):
        z = jnp.where(lane == c, col, z)
    info_ref[0] = z.T[0:_INFO_ROWS, :].astype(jnp.int32)
    wts_ref[...] = jnp.where(lane == 0, w1, jnp.where(lane == 1, w2, 0.0))


def _positions_kernel(base_ref, info_ref, pos_ref, *, n_experts):
    info = info_ref[0]
    expert = info[0:2]
    start = jnp.zeros_like(expert)
    for k in range(n_experts):
        start = jnp.where(expert == k, base_ref[k], start)
    pos_ref[0] = jnp.concatenate([start + info[2:4], info[2:_INFO_ROWS]], axis=0)


def _row_positions(info_ref, r):
    return info_ref[0, r], info_ref[1, r]


def _load_info(info_hbm, info_ref, sem):
    cp = pltpu.make_async_copy(info_hbm.at[pl.program_id(0)], info_ref, sem)
    cp.start()
    cp.wait()


def _scatter_kernel(ends_ref, info_hbm, hp_ref, hs_ref, info_ref, zero_ref, isem, sem,
                    *, n_experts, n_tiles):
    @pl.when(pl.program_id(0) == 0)
    def _():
        zero_ref[...] = jnp.zeros_like(zero_ref)

        def zero_tile(row0):
            dst = hs_ref.at[pl.ds(pl.multiple_of(row0 * SUBLANES, SUBLANES), MOE_TILE * SUBLANES), :]
            return pltpu.make_async_copy(zero_ref, dst, sem)

        def issue_e(e, issued):
            end = ends_ref[e]
            start = jnp.where(e == 0, 0, ends_ref[jnp.maximum(e - 1, 0)])
            nonempty = end > start

            @pl.when(nonempty)
            def _():
                zero_tile(pl.multiple_of(end - MOE_TILE, MOE_TILE)).start()

            return issued + nonempty.astype(jnp.int32)

        def issue_tail(s, carry):
            zero_tile(pl.multiple_of(s * MOE_TILE, MOE_TILE)).start()
            return carry

        first_tail = ends_ref[n_experts - 1] // MOE_TILE
        issued = lax.fori_loop(0, n_experts, issue_e, 0)
        lax.fori_loop(first_tail, n_tiles, issue_tail, 0)

        def drain_zero(s, carry):
            zero_tile(0).wait()
            return carry

        lax.fori_loop(0, issued + n_tiles - first_tail, drain_zero, 0)

    _load_info(info_hbm, info_ref, isem)

    def row_copy(r, pos):
        return pltpu.make_async_copy(_row_tile(hp_ref, r, SUBLANES), _row_tile(hs_ref, pos, SUBLANES),
                                     sem)

    def issue(r, carry):
        for pos in _row_positions(info_ref, r):
            row_copy(r, pos).start()
        return carry

    def drain(r, carry):
        row_copy(0, 0).wait()
        row_copy(0, 0).wait()
        return carry

    lax.fori_loop(0, MOVE_TILE, issue, 0, unroll=ROW_COPY_UNROLL)
    lax.fori_loop(0, MOVE_TILE, drain, 0, unroll=2 * ROW_COPY_UNROLL)


def _expert_kernel(te_ref, hs_ref, wg_ref, wu_ref, wd_ref, ys_ref):
    del te_ref
    h = _load_row_tiles(hs_ref, SUBLANES).astype(BF16)
    gt = jnp.dot(h, wg_ref[0, 0].astype(BF16), preferred_element_type=F32)
    up = jnp.dot(h, wu_ref[0, 0].astype(BF16), preferred_element_type=F32)
    act = (gt / (1.0 + jnp.exp(-gt))) * up
    y = jnp.dot(act.astype(BF16), wd_ref[0, 0].astype(BF16), preferred_element_type=F32)
    _store_row_tiles(ys_ref, y)


def _combine_kernel(info_hbm, x_ref, wts_ref, mod_ref, fg_ref, ys_ref, o_ref,
                    info_ref, y1_ref, y2_ref, isem, sem, *, final_norm):
    _load_info(info_hbm, info_ref, isem)

    def row_copy(pos, dst_ref, r):
        return pltpu.make_async_copy(_row_tile(ys_ref, pos, SUBLANES), _row_tile(dst_ref, r, SUBLANES),
                                     sem)

    def issue(r, carry):
        p1, p2 = _row_positions(info_ref, r)
        row_copy(p1, y1_ref, r).start()
        row_copy(p2, y2_ref, r).start()
        return carry

    def drain(r, carry):
        row_copy(0, y1_ref, 0).wait()
        row_copy(0, y2_ref, 0).wait()
        return carry

    lax.fori_loop(0, MOVE_TILE, issue, 0, unroll=ROW_COPY_UNROLL)
    lax.fori_loop(0, MOVE_TILE, drain, 0, unroll=2 * ROW_COPY_UNROLL)
    wts = wts_ref[...]
    y = (wts[:, 0:1] * _load_row_tiles(y1_ref, SUBLANES)
         + wts[:, 1:2] * _load_row_tiles(y2_ref, SUBLANES))
    xn = x_ref[...] + mod_ref[0][5:6] * y
    if final_norm:
        r = lax.rsqrt(jnp.mean(xn * xn, axis=-1, keepdims=True) + RMS_EPS)
        xn = (xn * r) * fg_ref[...]
    o_ref[...] = xn


def _moe_layer(x2, mod, norm_g, w_group, w_expert, w_gate, w_up, w_down, layer, final_g, seq,
               final_norm):
    t, d = x2.shape
    _, e, _, f = w_gate.shape
    assert d == SUBLANES * LANES, "a token row must fill exactly one (8, 128) tile"
    n_move = t // MOVE_TILE
    route_per_move = MOVE_TILE // ROUTE_TILE
    n_tiles = (2 * t) // MOE_TILE + e
    wr = jnp.zeros((d, LANES), F32).at[:, :e].set(w_expert).at[:, e:e + N_EXPERT_GROUPS].set(w_group)
    anyspace = pl.BlockSpec(memory_space=pl.ANY)

    hp, info, wts, cnt = pl.pallas_call(
        _route_kernel,
        grid=(t // ROUTE_TILE,),
        in_specs=[pl.BlockSpec((ROUTE_TILE, d), lambda ti: (ti, 0)),
                  pl.BlockSpec((1, 6, d), lambda ti: (ti * ROUTE_TILE // seq, 0, 0)),
                  pl.BlockSpec((1, d), lambda ti: (0, 0)),
                  pl.BlockSpec((d, LANES), lambda ti: (0, 0))],
        out_specs=[pl.BlockSpec((ROUTE_TILE * SUBLANES, LANES), lambda ti: (ti, 0)),
                   pl.BlockSpec((1, _INFO_ROWS, ROUTE_TILE),
                                lambda ti: (ti // route_per_move, 0, ti % route_per_move)),
                   pl.BlockSpec((ROUTE_TILE, LANES), lambda ti: (ti, 0)),
                   pl.BlockSpec((1, LANES), lambda ti: (0, 0))],
        out_shape=[jax.ShapeDtypeStruct((t * SUBLANES, LANES), F32),
                   jax.ShapeDtypeStruct((n_move, _INFO_ROWS, MOVE_TILE), jnp.int32),
                   jax.ShapeDtypeStruct((t, LANES), F32),
                   jax.ShapeDtypeStruct((1, LANES), F32)],
        scratch_shapes=[pltpu.VMEM((1, LANES), F32)],
        compiler_params=_cparams(1),
        name="moe_route",
    )(x2, mod, norm_g.reshape(1, d), wr)

    counts = cnt[0, :e].astype(jnp.int32)
    padded = (counts + MOE_TILE - 1) // MOE_TILE * MOE_TILE
    ends = jnp.cumsum(padded)
    base = ends - padded
    tile_start = jnp.arange(n_tiles, dtype=jnp.int32) * MOE_TILE
    tile_expert = jnp.minimum(jnp.sum((ends[None, :] <= tile_start[:, None]).astype(jnp.int32), axis=1),
                              e - 1)

    info_blk = pl.BlockSpec((1, _INFO_ROWS, MOVE_TILE), lambda ti, b: (ti, 0, 0))
    pos = pl.pallas_call(
        functools.partial(_positions_kernel, n_experts=e),
        grid_spec=pltpu.PrefetchScalarGridSpec(
            num_scalar_prefetch=1, grid=(n_move,), in_specs=[info_blk], out_specs=info_blk),
        out_shape=jax.ShapeDtypeStruct(info.shape, jnp.int32),
        compiler_params=_cparams(1),
        name="moe_positions",
    )(base, info)

    hs = pl.pallas_call(
        functools.partial(_scatter_kernel, n_experts=e, n_tiles=n_tiles),
        grid_spec=pltpu.PrefetchScalarGridSpec(
            num_scalar_prefetch=1,
            grid=(n_move,),
            in_specs=[anyspace,
                      pl.BlockSpec((MOVE_TILE * SUBLANES, LANES), lambda ti, en: (ti, 0))],
            out_specs=anyspace,
            scratch_shapes=[pltpu.SMEM((_INFO_ROWS, MOVE_TILE), jnp.int32),
                            pltpu.VMEM((MOE_TILE * SUBLANES, LANES), F32),
                            pltpu.SemaphoreType.DMA, pltpu.SemaphoreType.DMA]),
        out_shape=jax.ShapeDtypeStruct((n_tiles * MOE_TILE * SUBLANES, LANES), F32),
        compiler_params=_cparams(1),
        name="moe_scatter",
    )(ends, pos, hp)

    ys = pl.pallas_call(
        _expert_kernel,
        grid_spec=pltpu.PrefetchScalarGridSpec(
            num_scalar_prefetch=1,
            grid=(n_tiles,),
            in_specs=[pl.BlockSpec((MOE_TILE * SUBLANES, LANES), lambda si, te: (si, 0)),
                      pl.BlockSpec((1, 1, d, f), lambda si, te: (layer, te[si], 0, 0)),
                      pl.BlockSpec((1, 1, d, f), lambda si, te: (layer, te[si], 0, 0)),
                      pl.BlockSpec((1, 1, f, d), lambda si, te: (layer, te[si], 0, 0))],
            out_specs=pl.BlockSpec((MOE_TILE * SUBLANES, LANES), lambda si, te: (si, 0))),
        out_shape=jax.ShapeDtypeStruct((n_tiles * MOE_TILE * SUBLANES, LANES), F32),
        compiler_params=_cparams(1),
        name="moe_experts",
    )(tile_expert, hs, w_gate, w_up, w_down)

    return pl.pallas_call(
        functools.partial(_combine_kernel, final_norm=final_norm),
        grid=(n_move,),
        in_specs=[anyspace,
                  pl.BlockSpec((MOVE_TILE, d), lambda ti: (ti, 0)),
                  pl.BlockSpec((MOVE_TILE, LANES), lambda ti: (ti, 0)),
                  pl.BlockSpec((1, 6, d), lambda ti: (ti * MOVE_TILE // seq, 0, 0)),
                  pl.BlockSpec((1, d), lambda ti: (0, 0)),
                  anyspace],
        out_specs=pl.BlockSpec((MOVE_TILE, d), lambda ti: (ti, 0)),
        scratch_shapes=[pltpu.SMEM((_INFO_ROWS, MOVE_TILE), jnp.int32),
                        pltpu.VMEM((MOVE_TILE * SUBLANES, LANES), F32),
                        pltpu.VMEM((MOVE_TILE * SUBLANES, LANES), F32),
                        pltpu.SemaphoreType.DMA, pltpu.SemaphoreType.DMA],
        out_shape=jax.ShapeDtypeStruct((t, d), F32),
        compiler_params=_cparams(1),
        name="moe_combine_final" if final_norm else "moe_combine",
    )(pos, x2, wts, mod, final_g.reshape(1, d), ys)


def _qkv_kernel(x_ref, mod_ref, g_ref, w_ref, q_ref, k_ref, v_ref):
    mod = mod_ref[0]
    h = _rms_mod(x_ref[0], g_ref[...], mod[1:2], mod[0:1]).astype(BF16)
    d = h.shape[1]
    q_ref[0] = jnp.dot(h, w_ref[:, 0:d], preferred_element_type=F32)
    k_ref[0] = jnp.dot(h, w_ref[:, d:2 * d], preferred_element_type=F32).astype(BF16)
    v_ref[0] = jnp.dot(h, w_ref[:, 2 * d:3 * d], preferred_element_type=F32).astype(BF16)


def _qkv(x, mod, norm_g, w_qkv):
    b, s, d = x.shape
    ts = 512
    blk = pl.BlockSpec((1, ts, d), lambda bi, si: (bi, si, 0))
    return pl.pallas_call(
        _qkv_kernel,
        grid=(b, s // ts),
        in_specs=[blk,
                  pl.BlockSpec((1, 6, d), lambda bi, si: (bi, 0, 0)),
                  pl.BlockSpec((1, d), lambda bi, si: (0, 0)),
                  pl.BlockSpec((d, 3 * d), lambda bi, si: (0, 0))],
        out_specs=[blk, blk, blk],
        out_shape=[jax.ShapeDtypeStruct((b, s, d), F32),
                   jax.ShapeDtypeStruct((b, s, d), BF16),
                   jax.ShapeDtypeStruct((b, s, d), BF16)],
        compiler_params=_cparams(2),
        name="qkv_proj",
    )(x, mod, norm_g.reshape(1, d), w_qkv.astype(BF16))


_SLOPE_LANE = 64
_BLKIDX_LANE = 67
_PAD_LANE = 70
PAD_BLOCKS = 2 * ATTN_GROUP
UNDERFLOW_LOG2 = -140.0
NORM_SLACK = 1.02


def _attn_kernel(slopes_ref, islopes_ref, q_ref, qnext_ref, k_ref, v_ref, o_ref,
                 kmean_ref, knorm_ref, kext_ref, v0_ref, v1_ref, lhs_ref, lhs_next_ref,
                 qnorm_ref, qnorm_next_ref, sa_ref, sb_ref, m_ref, acc_ref, *, nb):
    p = pl.program_id(1)
    i = pl.program_id(2)
    blk = MOBA_BLOCK
    pad = PAD_BLOCKS * blk
    lane = lax.broadcasted_iota(jnp.int32, (blk, LANES), 1)
    head0 = lane < HEAD_DIM

    nt = (((1,), (1,)), ((), ()))
    gw = ATTN_GROUP * blk
    lane2 = lax.broadcasted_iota(jnp.int32, (2 * blk, LANES), 1)
    top_half = lax.broadcasted_iota(jnp.int32, (2 * blk, LANES), 0) < blk

    def build_lhs(qblk, own, lhs_dst, qnorm_dst):
        q = qblk * (HEAD_DIM ** -0.5 * LOG2E)
        qs = jnp.concatenate([jnp.where(head0, q, 0.0), jnp.where(head0, 0.0, q)], axis=0)
        gate = lax.dot_general(kmean_ref[0:HEAD_DIM, :], qs, nt, precision=HIGHEST,
                               preferred_element_type=F32)
        blk_i = lax.broadcasted_iota(jnp.int32, gate.shape, 0)
        blk_f = blk_i.astype(F32)
        valid = blk_i < own
        gate = jnp.where(valid, gate, -jnp.inf)
        sel = blk_i == own
        for _ in range(MOBA_TOPK):
            mx = jnp.max(gate, axis=0, keepdims=True)
            idx = jnp.min(jnp.where(gate == mx, blk_f, 1e9), axis=0, keepdims=True)
            hit = blk_f == idx
            sel = sel | (hit & valid)
            gate = jnp.where(hit, -jnp.inf, gate)
        bias = jnp.where(sel, 0.0, NEG_BIG)
        ext = jnp.concatenate([bias, jnp.full(bias.shape, NEG_BIG, F32)], axis=0).T
        slope = jnp.where(top_half, slopes_ref[2 * p], slopes_ref[2 * p + 1]) * LOG2E
        s_hi = slope.astype(BF16).astype(F32)
        rem = slope - s_hi
        s_mid = rem.astype(BF16).astype(F32)
        s_lo = rem - s_mid
        for k, part in enumerate((s_hi, s_mid, s_lo)):
            ext = jnp.where(lane2 == _SLOPE_LANE + k, part, ext)
            ext = jnp.where(lane2 == _BLKIDX_LANE + k, part * float(blk), ext)
        ext = jnp.where(lane2 > _PAD_LANE, 0.0, ext)
        lhs_dst[:, 0:LANES] = qs.astype(BF16)
        lhs_dst[:, LANES:2 * LANES] = ext.astype(BF16)
        qnorm_dst[...] = jnp.sqrt(jnp.sum(qs * qs, axis=-1, keepdims=True))

    @pl.when(i == 0)
    def _():
        kmean_ref[...] = jnp.zeros_like(kmean_ref)
        knorm_ref[...] = jnp.zeros_like(knorm_ref)
        kext_ref[0:pad, 0:LANES] = jnp.zeros((pad, LANES), BF16)
        lane_p = lax.broadcasted_iota(jnp.int32, (pad, LANES), 1)
        kext_ref[0:pad, LANES:2 * LANES] = jnp.where(lane_p == _PAD_LANE, 1.0, 0.0).astype(BF16)
        v0_ref[0:pad, :] = jnp.zeros((pad, LANES), BF16)
        v1_ref[0:pad, :] = jnp.zeros((pad, LANES), BF16)
        row = lax.broadcasted_iota(jnp.int32, (blk, LANES), 0).astype(F32)
        in_slope = (lane >= _SLOPE_LANE) & (lane < _SLOPE_LANE + 3)
        in_blk = (lane >= _BLKIDX_LANE) & (lane < _BLKIDX_LANE + 3)
        ones0 = jnp.where(lane == HEAD_DIM, 1.0, 0.0)
        ones1 = jnp.where(lane == 0, 1.0, 0.0)

        def build(j, carry):
            src = pl.ds(pl.multiple_of(j * blk, blk), blk)
            dst = pl.ds(pl.multiple_of(j * blk + pad, blk), blk)
            kj = k_ref[0, src, :]
            kf = kj.astype(F32)
            kmean_ref[pl.ds(j, 1), :] = jnp.mean(kf, axis=0, keepdims=True)
            ksq = kf * kf
            n0 = jnp.max(jnp.sum(jnp.where(head0, ksq, 0.0), axis=-1, keepdims=True),
                         axis=0, keepdims=True)
            n1 = jnp.max(jnp.sum(jnp.where(head0, 0.0, ksq), axis=-1, keepdims=True),
                         axis=0, keepdims=True)
            knorm_ref[0:1, :] = jnp.maximum(knorm_ref[0:1, :], n0)
            knorm_ref[1:2, :] = jnp.maximum(knorm_ref[1:2, :], n1)
            jf = jnp.full((blk, LANES), j, jnp.int32).astype(F32)
            extra = jnp.where(lane == j, 1.0, jnp.where(in_slope, row, jnp.where(in_blk, jf, 0.0)))
            kext_ref[dst, 0:LANES] = kj
            kext_ref[dst, LANES:2 * LANES] = extra.astype(BF16)
            vj = v_ref[0, src, :].astype(F32)
            v0_ref[dst, :] = jnp.where(head0, vj, ones0).astype(BF16)
            v1_ref[dst, :] = jnp.where(head0, ones1, vj).astype(BF16)
            return carry

        lax.fori_loop(0, nb, build, 0, unroll=2)

        build_lhs(q_ref[0], i, lhs_next_ref, qnorm_next_ref)

    lhs_ref[...] = lhs_next_ref[...]
    qnorm_ref[...] = qnorm_next_ref[...]
    m_ref[...] = jnp.full(m_ref.shape, NEG_BIG, F32)
    acc_ref[...] = jnp.zeros_like(acc_ref)

    def group_rows(t):
        start = (i - ATTN_GROUP * (t + 1) + 1 + PAD_BLOCKS) * blk
        return pl.ds(pl.multiple_of(start, blk), gw)

    def scores(t, dst_ref):
        dst_ref[...] = lax.dot_general(lhs_ref[...], kext_ref[group_rows(t), :], nt,
                                       preferred_element_type=F32)

    def softmax_group(t, src_ref, own_block=False):
        rows = group_rows(t)
        for hh in range(2):
            r = slice(hh * blk, (hh + 1) * blk)
            s = src_ref[r, :]
            if own_block:
                qi = lax.broadcasted_iota(jnp.int32, (blk, blk), 0)
                ki = lax.broadcasted_iota(jnp.int32, (blk, blk), 1)
                s = jnp.concatenate([s[:, :gw - blk],
                                     jnp.where(ki <= qi, s[:, gw - blk:], NEG_BIG)], axis=1)
            m_old = m_ref[r, :]
            m_new = jnp.maximum(m_old, jnp.max(s, axis=-1, keepdims=True))
            alpha = jnp.exp2(m_old - m_new)
            pm = jnp.exp2(s - m_new).astype(BF16)
            vref = v0_ref if hh == 0 else v1_ref
            pv = jnp.dot(pm, vref[rows, :], preferred_element_type=F32)
            acc_ref[r, :] = alpha * acc_ref[r, :] + pv
            m_ref[r, :] = m_new

    scores(0, sa_ref)
    scores(1, sb_ref)
    build_lhs(qnext_ref[0], i + 1, lhs_next_ref, qnorm_next_ref)
    softmax_group(0, sa_ref, own_block=True)

    knorm = jnp.sqrt(knorm_ref[0:2, 0:1])
    excess = (NORM_SLACK * qnorm_ref[...] * jnp.where(top_half[:, 0:1], knorm[0:1], knorm[1:2])
              - m_ref[...])
    a0 = jnp.max(excess[0:blk], axis=0, keepdims=True)
    a1 = jnp.max(excess[blk:2 * blk], axis=0, keepdims=True)
    edge = (blk - 1.0) / blk
    x0 = (UNDERFLOW_LOG2 - a0) * islopes_ref[2 * p] - edge
    x1 = (UNDERFLOW_LOG2 - a1) * islopes_ref[2 * p + 1] - edge
    i_f = jnp.full((1, 1), i, jnp.int32).astype(F32)
    first_needed = jnp.clip(jnp.floor(jnp.minimum(x0, x1)), 0.0, i_f)
    left = jnp.maximum(i_f - first_needed - (ATTN_GROUP - 1.0), 0.0)
    ngroups = jnp.ceil(left * (1.0 / ATTN_GROUP)).astype(jnp.int32)[0, 0]
    npair = ngroups // 2

    def body(u, carry):
        scores(2 * u + 2, sa_ref)
        softmax_group(2 * u + 1, sb_ref)
        scores(jnp.minimum(2 * u + 3, ngroups), sb_ref)
        softmax_group(2 * u + 2, sa_ref)
        return carry

    lax.fori_loop(0, npair, body, 0)

    @pl.when((ngroups & 1) == 1)
    def _():
        softmax_group(ngroups, sb_ref)

    a0 = acc_ref[0:blk, :]
    a1 = acc_ref[blk:2 * blk, :]
    l0 = a0[:, HEAD_DIM:HEAD_DIM + 1]
    l1 = a1[:, 0:1]
    o_ref[0] = jnp.where(head0, a0 / l0, a1 / l1).astype(o_ref.dtype)


def _attention(q, k, v):
    b, s, d = q.shape
    blk = MOBA_BLOCK
    nb = s // blk
    assert s % blk == 0 and nb <= HEAD_DIM, "block-select lanes hold at most 64 blocks"
    slopes = 2.0 ** (-8.0 * jnp.arange(1, N_HEADS + 1, dtype=F32) / N_HEADS)
    islopes = 1.0 / (slopes * (LOG2E * blk))
    pairs = d // LANES
    sp = s + PAD_BLOCKS * blk
    full = pl.BlockSpec((1, s, LANES), lambda bi, pi, ii: (bi, 0, pi))
    tile = pl.BlockSpec((1, blk, LANES), lambda bi, pi, ii: (bi, ii, pi))
    first_tile = pl.BlockSpec((1, blk, LANES), lambda bi, pi, ii: (bi, 0, pi))
    next_tile = pl.BlockSpec((1, blk, LANES), lambda bi, pi, ii: (bi, jnp.minimum(ii + 1, nb - 1), pi))
    smem = pl.BlockSpec(memory_space=pltpu.SMEM)
    return pl.pallas_call(
        functools.partial(_attn_kernel, nb=nb),
        grid=(b, pairs, nb),
        in_specs=[smem, smem, first_tile, next_tile, full, full],
        out_specs=tile,
        out_shape=jax.ShapeDtypeStruct((b, s, d), BF16),
        scratch_shapes=[pltpu.VMEM((LANES, LANES), F32),
                        pltpu.VMEM((8, LANES), F32),
                        pltpu.VMEM((sp, 2 * LANES), BF16),
                        pltpu.VMEM((sp, LANES), BF16),
                        pltpu.VMEM((sp, LANES), BF16),
                        pltpu.VMEM((2 * blk, 2 * LANES), BF16),
                        pltpu.VMEM((2 * blk, 2 * LANES), BF16),
                        pltpu.VMEM((2 * blk, 1), F32),
                        pltpu.VMEM((2 * blk, 1), F32),
                        pltpu.VMEM((2 * blk, ATTN_GROUP * blk), F32),
                        pltpu.VMEM((2 * blk, ATTN_GROUP * blk), F32),
                        pltpu.VMEM((2 * blk, 1), F32),
                        pltpu.VMEM((2 * blk, LANES), F32)],
        compiler_params=_cparams(3),
        name="moba_attn",
    )(slopes, islopes, q, q, k, v)


def _oproj_kernel(o_ref, x_ref, mod_ref, w_ref, out_ref):
    y = jnp.dot(o_ref[0], w_ref[...], preferred_element_type=F32)
    out_ref[0] = x_ref[0] + mod_ref[0][2:3] * y


def _oproj(o, x, mod, w_o):
    b, s, d = x.shape
    ts = 512
    blk = pl.BlockSpec((1, ts, d), lambda bi, si: (bi, si, 0))
    return pl.pallas_call(
        _oproj_kernel,
        grid=(b, s // ts),
        in_specs=[blk, blk,
                  pl.BlockSpec((1, 6, d), lambda bi, si: (bi, 0, 0)),
                  pl.BlockSpec((d, d), lambda bi, si: (0, 0))],
        out_specs=blk,
        out_shape=jax.ShapeDtypeStruct((b, s, d), F32),
        compiler_params=_cparams(2),
        name="attn_oproj",
    )(o, x, mod, w_o.astype(BF16))


def kernel(x, c, ada_w, ada_b, norm_mix_g, norm_ffn_g, pool_w, pool_scale, w_qkv, w_o,
           router_group_w, router_expert_w, exp_w_gate, exp_w_up, exp_w_down, final_norm_g):
    b, s, d = x.shape
    depth = ada_w.shape[0]
    mod = _adaln(c, ada_w, ada_b)
    for i in range(depth):
        if i % 2 == 0:
            x = _pool_layer(x, mod[i], norm_mix_g[i], pool_w[i // 2], pool_scale[i // 2])
        else:
            q, k, v = _qkv(x, mod[i], norm_mix_g[i], w_qkv[i // 2])
            o = _attention(q, k, v)
            x = _oproj(o, x, mod[i], w_o[i // 2])
        x = _moe_layer(x.reshape(b * s, d), mod[i], norm_ffn_g[i], router_group_w[i],
                       router_expert_w[i], exp_w_gate, exp_w_up, exp_w_down, i,
                       final_norm_g, s, final_norm=(i == depth - 1)).reshape(b, s, d)
    return x
```
